```python
import math
import jax, jax.numpy as jnp
from jax import lax
import numpy as np

D_MODEL = 1024
BATCH = 8
SEQ = 4096
DEPTH = 2
DEC_BATCH = 16
DEC_SEQ = 4096
PAST_LEN = 128

HEAD_DIM = 64
MIX_WIDTH = D_MODEL
H_A = 4
H_B = 4
H_B_KV = 2
H_C = 4
H_D = 4
DILATED_PATTERNS = ((128, 1), (512, 4), (2048, 16))
T5_BUCKETS = 32
T5_MAX_DIST = 1024
GRID_W = 64
NA_ROWS = 8
NA_COLS = 16
D_Q_LORA = 256
D_KV_LORA = 128
D_NOPE = 64
D_ROPE = 32
D_V = 64
ROPE_THETA = 10000.0
QBLOCK = 128
D_FF = -(-8 * D_MODEL // (3 * 256)) * 256
IN_SIZES = (H_A * HEAD_DIM, H_A * HEAD_DIM, H_A * HEAD_DIM,
            H_B * HEAD_DIM, H_B_KV * HEAD_DIM, H_B_KV * HEAD_DIM,
            H_C * HEAD_DIM, H_C * HEAD_DIM, H_C * HEAD_DIM,
            D_Q_LORA, D_KV_LORA, D_ROPE)
D_IN = sum(IN_SIZES)
N_GROUPS = 4
GROUP_WIDTH = MIX_WIDTH // N_GROUPS
RMS_EPS = 1e-6
NEG_INF = -1e30

kernel_name = "hybrid_parallel_head_group_encoder"


def rms_norm(x, g):
    xf = x.astype(jnp.float32)
    y = xf * lax.rsqrt(jnp.mean(xf * xf, axis=-1, keepdims=True) + RMS_EPS)
    return (y * g.astype(jnp.float32)).astype(x.dtype)


def rope_angles(pos, dim):
    inv = 1.0 / (ROPE_THETA ** (jnp.arange(0, dim, 2, dtype=jnp.float32) / dim))
    return pos.astype(jnp.float32)[:, None] * inv[None, :]


def apply_rope(x, ang):
    x1, x2 = jnp.split(x, 2, axis=-1)
    cos = jnp.cos(ang)[:, None, :].astype(x.dtype)
    sin = jnp.sin(ang)[:, None, :].astype(x.dtype)
    return jnp.concatenate([x1 * cos - x2 * sin, x1 * sin + x2 * cos], axis=-1)


def t5_bucket(rel):
    nb = T5_BUCKETS // 2
    max_exact = nb // 2
    n = jnp.abs(rel)
    n_f = jnp.maximum(n, max_exact).astype(jnp.float32)
    large = max_exact + (jnp.log(n_f / max_exact) / math.log(T5_MAX_DIST / max_exact)
                         * (nb - max_exact)).astype(jnp.int32)
    large = jnp.minimum(large, nb - 1)
    return jnp.where(rel > 0, nb, 0) + jnp.where(n < max_exact, n, large)


def dilated_window_attention(q, k, v, t5_bias):
    B, S, H, Dh = q.shape
    nblk = S // QBLOCK
    scale = Dh ** -0.5
    outs, lses = [], []
    for window, dil in DILATED_PATTERNS:
        half = window // (2 * dil)
        off = jnp.arange(-half, half + 1, dtype=jnp.int32) * dil
        bias = t5_bias[t5_bucket(off)].T.astype(jnp.float32)

        def block(i, off=off, bias=bias):
            qpos = i * QBLOCK + jnp.arange(QBLOCK, dtype=jnp.int32)
            kidx = qpos[:, None] + off[None, :]
            valid = (kidx >= 0) & (kidx < S)
            kidx = jnp.clip(kidx, 0, S - 1)
            qb = lax.dynamic_slice_in_dim(q, i * QBLOCK, QBLOCK, axis=1)
            kg = k[:, kidx]
            vg = v[:, kidx]
            s = jnp.einsum('bqhd,bqkhd->bhqk', qb, kg).astype(jnp.float32) * scale + bias[:, None, :]
            s = jnp.where(valid[None, None], s, NEG_INF)
            m = jnp.max(s, axis=-1, keepdims=True)
            e = jnp.exp(s - m)
            l = jnp.sum(e, axis=-1, keepdims=True)
            o = jnp.einsum('bhqk,bqkhd->bqhd', e.astype(v.dtype), vg).astype(jnp.float32)
            o = o / jnp.transpose(l, (0, 2, 1, 3))
            lse = (m + jnp.log(l))[..., 0]
            return o, lse

        o, lse = lax.map(block, jnp.arange(nblk))
        outs.append(jnp.moveaxis(o, 0, 1).reshape(B, S, H, Dh))
        lses.append(jnp.transpose(lse, (1, 2, 0, 3)).reshape(B, H, S))
    w = jax.nn.softmax(jnp.stack(lses, axis=0), axis=0)
    w = jnp.transpose(w, (0, 1, 3, 2))[..., None]
    return jnp.sum(w * jnp.stack(outs, axis=0), axis=0)


def dense_block_attention(q, k, v):
    B, S, G, R, Dq = q.shape
    Dv = v.shape[-1]
    scale = Dq ** -0.5

    def block(i):
        qb = lax.dynamic_slice_in_dim(q, i * QBLOCK, QBLOCK, axis=1)
        s = jnp.einsum('bqgrd,bkgd->bgrqk', qb, k).astype(jnp.float32) * scale
        p = jax.nn.softmax(s, axis=-1).astype(v.dtype)
        return jnp.einsum('bgrqk,bkgd->bqgrd', p, v)

    o = lax.map(block, jnp.arange(S // QBLOCK))
    return jnp.moveaxis(o, 0, 1).reshape(B, S, G * R, Dv)


def neighborhood_attention(q, k, v, rpb):
    B, S, H, Dh = q.shape
    rows = S // GRID_W
    kr = min(NA_ROWS, rows)
    scale = Dh ** -0.5
    qg = q.reshape(B, rows, GRID_W, H, Dh)
    kg = k.reshape(B, rows, GRID_W, H, Dh)
    vg = v.reshape(B, rows, GRID_W, H, Dh)
    cols = jnp.arange(GRID_W, dtype=jnp.int32)
    cs = jnp.clip(cols - NA_COLS // 2, 0, GRID_W - NA_COLS)
    cidx = cs[:, None] + jnp.arange(NA_COLS, dtype=jnp.int32)[None, :]
    dc = cidx - cols[:, None]

    def row_block(r):
        rs = jnp.clip(r - kr // 2, 0, rows - kr)
        k_rows = lax.dynamic_slice_in_dim(kg, rs, kr, axis=1)
        v_rows = lax.dynamic_slice_in_dim(vg, rs, kr, axis=1)
        k_n = k_rows[:, :, cidx]
        v_n = v_rows[:, :, cidx]
        q_r = lax.dynamic_index_in_dim(qg, r, axis=1, keepdims=False)
        dr = rs + jnp.arange(kr, dtype=jnp.int32) - r
        bias = rpb[:, dr[None, :, None] + NA_ROWS - 1, dc[:, None, :] + NA_COLS - 1]
        s = jnp.einsum('bchd,brckhd->bhcrk', q_r, k_n).astype(jnp.float32) * scale
        s = s + bias.astype(jnp.float32)[None]
        p = jax.nn.softmax(s.reshape(B, H, GRID_W, kr * NA_COLS), axis=-1)
        p = p.reshape(B, H, GRID_W, kr, NA_COLS).astype(v.dtype)
        return jnp.einsum('bhcrk,brckhd->bchd', p, v_n)

    o = lax.map(row_block, jnp.arange(rows))
    return jnp.moveaxis(o, 0, 1).reshape(B, S, H, Dh)


def mixing_layer(h, t5_bias, w_in, b_q_gain, b_k_gain, c_rpb, d_q_gain, d_w_uq,
                 d_kv_gain, d_w_ukv, out_gain, w_out):
    B, S, _ = h.shape
    proj = h @ w_in
    splits = [int(c) for c in np.cumsum(IN_SIZES)[:-1]]
    aq, ak, av, bq, bk, bv, cq, ck, cv, dq, dkv, dkr = jnp.split(proj, splits, axis=-1)

    def heads(t, n):
        return t.reshape(B, S, n, t.shape[-1] // n)

    t = jnp.arange(S, dtype=jnp.int32)

    o_a = dilated_window_attention(heads(aq, H_A), heads(ak, H_A), heads(av, H_A), t5_bias)

    qb = rms_norm(heads(bq, H_B), b_q_gain)
    kb = rms_norm(heads(bk, H_B_KV), b_k_gain)
    ang_r = rope_angles(t // GRID_W, HEAD_DIM // 2)
    ang_c = rope_angles(t % GRID_W, HEAD_DIM // 2)
    hd2 = HEAD_DIM // 2
    qb = jnp.concatenate([apply_rope(qb[..., :hd2], ang_r), apply_rope(qb[..., hd2:], ang_c)], axis=-1)
    kb = jnp.concatenate([apply_rope(kb[..., :hd2], ang_r), apply_rope(kb[..., hd2:], ang_c)], axis=-1)
    qb = qb.reshape(B, S, H_B_KV, H_B // H_B_KV, HEAD_DIM)
    o_b = dense_block_attention(qb, kb, heads(bv, H_B_KV))

    o_c = neighborhood_attention(heads(cq, H_C), heads(ck, H_C), heads(cv, H_C), c_rpb)

    ang_t = rope_angles(t, D_ROPE)
    q_d = (rms_norm(dq, d_q_gain) @ d_w_uq).reshape(B, S, H_D, D_NOPE + D_ROPE)
    q_d = jnp.concatenate([q_d[..., :D_NOPE], apply_rope(q_d[..., D_NOPE:], ang_t)], axis=-1)
    kv_d = (rms_norm(dkv, d_kv_gain) @ d_w_ukv).reshape(B, S, H_D, D_NOPE + D_V)
    k_rope = jnp.broadcast_to(apply_rope(dkr[:, :, None, :], ang_t), (B, S, H_D, D_ROPE))
    k_d = jnp.concatenate([kv_d[..., :D_NOPE], k_rope], axis=-1)
    o_d = dense_block_attention(q_d[:, :, :, None, :], k_d, kv_d[..., D_NOPE:])

    o = jnp.stack([o_a.astype(h.dtype).reshape(B, S, GROUP_WIDTH),
                   o_b.reshape(B, S, GROUP_WIDTH),
                   o_c.reshape(B, S, GROUP_WIDTH),
                   o_d.reshape(B, S, GROUP_WIDTH)], axis=-2)
    o = rms_norm(o, out_gain.reshape(N_GROUPS, GROUP_WIDTH))
    return o.reshape(B, S, MIX_WIDTH) @ w_out


def swiglu(h, w_gate, w_up, w_down):
    return (jax.nn.silu(h @ w_gate) * (h @ w_up)) @ w_down


def trunk(x, t5_bias, norm_mix, w_in, b_q_gain, b_k_gain, c_rpb, d_q_gain, d_w_uq,
          d_kv_gain, d_w_ukv, out_gain, w_out, norm_ffn, w_gate, w_up, w_down, final_norm):
    for l in range(DEPTH):
        h = rms_norm(x, norm_mix[l])
        x = x + mixing_layer(h, t5_bias, w_in[l], b_q_gain[l], b_k_gain[l], c_rpb[l],
                             d_q_gain[l], d_w_uq[l], d_kv_gain[l], d_w_ukv[l],
                             out_gain[l], w_out[l])
        h = rms_norm(x, norm_ffn[l])
        x = x + swiglu(h, w_gate[l], w_up[l], w_down[l])
    return rms_norm(x, final_norm)


def setup_inputs(seed: int = 0) -> dict:
    key = jax.random.key(seed)
    ks = jax.random.split(key, 20)
    f32 = jnp.float32

    def nrm(k, shape, scale):
        return jax.random.normal(k, shape, f32) * scale

    def gain(k, shape):
        return 1.0 + 0.05 * jax.random.normal(k, shape, f32)

    return {
        "x_prompt": nrm(ks[0], (BATCH, SEQ, D_MODEL), 1.0),
        "x_sample": nrm(ks[1], (DEC_BATCH, DEC_SEQ, D_MODEL), 1.0),
        "t5_bias": nrm(ks[2], (T5_BUCKETS, H_A), 0.5),
        "norm_mix": gain(ks[3], (DEPTH, D_MODEL)),
        "w_in": nrm(ks[4], (DEPTH, D_MODEL, D_IN), D_MODEL ** -0.5),
        "b_q_gain": gain(ks[5], (DEPTH, HEAD_DIM)),
        "b_k_gain": gain(ks[6], (DEPTH, HEAD_DIM)),
        "c_rpb": nrm(ks[7], (DEPTH, H_C, 2 * NA_ROWS - 1, 2 * NA_COLS - 1), 0.5),
        "d_q_gain": gain(ks[8], (DEPTH, D_Q_LORA)),
        "d_w_uq": nrm(ks[9], (DEPTH, D_Q_LORA, H_D * (D_NOPE + D_ROPE)), D_Q_LORA ** -0.5),
        "d_kv_gain": gain(ks[10], (DEPTH, D_KV_LORA)),
        "d_w_ukv": nrm(ks[11], (DEPTH, D_KV_LORA, H_D * (D_NOPE + D_V)), D_KV_LORA ** -0.5),
        "out_gain": gain(ks[12], (DEPTH, MIX_WIDTH)),
        "w_out": nrm(ks[13], (DEPTH, MIX_WIDTH, D_MODEL), MIX_WIDTH ** -0.5),
        "norm_ffn": gain(ks[14], (DEPTH, D_MODEL)),
        "w_gate": nrm(ks[15], (DEPTH, D_MODEL, D_FF), D_MODEL ** -0.5),
        "w_up": nrm(ks[16], (DEPTH, D_MODEL, D_FF), D_MODEL ** -0.5),
        "w_down": nrm(ks[17], (DEPTH, D_FF, D_MODEL), D_FF ** -0.5),
        "final_norm": gain(ks[18], (D_MODEL,)),
    }


def reference(x_prompt, x_sample, t5_bias, norm_mix, w_in, b_q_gain, b_k_gain, c_rpb,
              d_q_gain, d_w_uq, d_kv_gain, d_w_ukv, out_gain, w_out, norm_ffn,
              w_gate, w_up, w_down, final_norm):
    y_prompt = trunk(x_prompt, t5_bias, norm_mix, w_in, b_q_gain, b_k_gain, c_rpb, d_q_gain,
                     d_w_uq, d_kv_gain, d_w_ukv, out_gain, w_out, norm_ffn, w_gate, w_up,
                     w_down, final_norm)
    y_sample = trunk(x_sample, t5_bias, norm_mix, w_in, b_q_gain, b_k_gain, c_rpb, d_q_gain,
                     d_w_uq, d_kv_gain, d_w_ukv, out_gain, w_out, norm_ffn, w_gate, w_up,
                     w_down, final_norm)
    return (y_prompt, y_sample)
```

```python
import functools
import math

import jax
import jax.numpy as jnp
from jax import lax
from jax.experimental import pallas as pl
from jax.experimental.pallas import tpu as pltpu

F32 = jnp.float32
BF16 = jnp.bfloat16

D_MODEL = 1024
SEQ = 4096
DEPTH = 2
HEAD_DIM = 64
N_HEADS = 4
GROUP_WIDTH = N_HEADS * HEAD_DIM
DILATED_PATTERNS = ((128, 1), (512, 4), (2048, 16))
BAND_HALF = 64
T5_BUCKETS = 32
T5_MAX_DIST = 1024
GRID_W = 64
GRID_ROWS = SEQ // GRID_W
NA_ROWS = 8
NA_COLS = 16
D_Q_LORA = 256
D_KV_LORA = 128
D_NOPE = 64
D_ROPE = 32
D_V = 64
ROPE_THETA = 10000.0
D_FF = 2816
RMS_EPS = 1e-6
NEG_INF = -1e30

LANES = 128
D_IN_PAD = 2560
COL_A, COL_B, COL_C, COL_D = 0, 768, 1280, 2048
ATTN_SCALE = HEAD_DIM ** -0.5
MLA_SCALE = (D_NOPE + D_ROPE) ** -0.5

TM_PROJ = 512
QBLK_A = 128
TQ_DENSE = 256
TK_DENSE = 512
ROWS_C = 4
TM_FFN = 256
FF_CHUNK = 1408
VMEM_LIMIT = 56 * 1024 * 1024


def _cparams(n_axes):
    return pltpu.CompilerParams(dimension_semantics=("arbitrary",) * n_axes,
                                vmem_limit_bytes=VMEM_LIMIT)


def _rms(x, g):
    return x * lax.rsqrt(jnp.mean(x * x, axis=-1, keepdims=True) + RMS_EPS) * g


def _dot(a, b):
    return jnp.dot(a, b, preferred_element_type=F32)


def _dot_nt(a, b):
    return lax.dot_general(a, b, (((1,), (1,)), ((), ())), preferred_element_type=F32)


def _rope(y, tab):
    return y * tab[0] + pltpu.roll(y, LANES - 16, 1) * tab[1] + pltpu.roll(y, 16, 1) * tab[2]


def _proj_kernel(x_ref, gmix_ref, w_ref, bqg_ref, bkg_ref, seg_ref, tabb_ref, tabd_ref,
                 dqg_ref, wuq_ref, dkvg_ref, wuk_ref, wuv_ref,
                 a_ref, bq_ref, bkv_ref, c_ref, dq_ref, dk_ref, dv_ref):
    h = _rms(x_ref[...], gmix_ref[...]).astype(BF16)

    def proj(lo, hi):
        return _dot(h, w_ref[:, lo:hi])

    pa = proj(COL_A, COL_B)
    a_ref[:, :GROUP_WIDTH] = (pa[:, :GROUP_WIDTH] * ATTN_SCALE).astype(BF16)
    a_ref[:, GROUP_WIDTH:] = pa[:, GROUP_WIDTH:].astype(BF16)

    pc = proj(COL_C, COL_D)
    c_ref[:, :GROUP_WIDTH] = (pc[:, :GROUP_WIDTH] * ATTN_SCALE).astype(BF16)
    c_ref[:, GROUP_WIDTH:] = pc[:, GROUP_WIDTH:].astype(BF16)

    pb = proj(COL_B, COL_C)
    seg = seg_ref[...]
    tabb = tabb_ref[...]

    def head_norm_rope(chunk, gain):
        sq = chunk * chunk
        hi = sq.astype(BF16)
        lo = (sq - hi.astype(F32)).astype(BF16)
        ms = _dot(hi, seg) + _dot(lo, seg)
        return _rope(chunk * lax.rsqrt(ms + RMS_EPS) * gain, tabb)

    for c in range(2):
        q = head_norm_rope(pb[:, c * LANES:(c + 1) * LANES], bqg_ref[...])
        bq_ref[:, c * LANES:(c + 1) * LANES] = (q * ATTN_SCALE).astype(BF16)
    bkv_ref[:, :LANES] = head_norm_rope(pb[:, 2 * LANES:3 * LANES], bkg_ref[...]).astype(BF16)
    bkv_ref[:, LANES:] = pb[:, 3 * LANES:].astype(BF16)

    pd = proj(COL_D, D_IN_PAD)
    tabd = tabd_ref[...]
    dq = _rms(pd[:, :D_Q_LORA], dqg_ref[...]).astype(BF16)
    qd = _dot(dq, wuq_ref[...])
    dkv = _rms(pd[:, D_Q_LORA:D_Q_LORA + D_KV_LORA], dkvg_ref[...]).astype(BF16)
    kn = _dot(dkv, wuk_ref[...])
    dv_ref[...] = _dot(dkv, wuv_ref[...]).astype(BF16)
    kr = _rope(pd[:, D_Q_LORA + D_KV_LORA:], tabd)
    for hd in range(N_HEADS):
        sl = slice(hd * LANES, (hd + 1) * LANES)
        dq_ref[:, sl] = (_rope(qd[:, sl], tabd) * MLA_SCALE).astype(BF16)
        dk_ref[:, sl] = (kn[:, sl] + kr).astype(BF16)


def _proj(x2d, lw, tabs):
    t = x2d.shape[0]
    blocks_per_seq = SEQ // TM_PROJ
    row = lambda i: (i, 0)
    const2 = lambda i: (0, 0)
    pos3 = lambda i: (0, i % blocks_per_seq, 0)

    def full(a):
        return pl.BlockSpec(a.shape, const2)

    out_widths = (768, 256, 256, 768, 512, 512, 256)
    return pl.pallas_call(
        _proj_kernel,
        grid=(t // TM_PROJ,),
        in_specs=[pl.BlockSpec((TM_PROJ, D_MODEL), row), full(lw["norm_mix"]), full(lw["w_in"]),
                  full(lw["b_q_gain"]), full(lw["b_k_gain"]), full(tabs["seg"]),
                  pl.BlockSpec((3, TM_PROJ, LANES), pos3), pl.BlockSpec((3, TM_PROJ, LANES), pos3),
                  full(lw["d_q_gain"]), full(lw["w_uq"]), full(lw["d_kv_gain"]), full(lw["w_uk"]),
                  full(lw["w_uv"])],
        out_specs=[pl.BlockSpec((TM_PROJ, w), row) for w in out_widths],
        out_shape=[jax.ShapeDtypeStruct((t, w), BF16) for w in out_widths],
        compiler_params=_cparams(1),
        name="proj",
    )(x2d, lw["norm_mix"], lw["w_in"], lw["b_q_gain"], lw["b_k_gain"], tabs["seg"], tabs["rope_b"],
      tabs["rope_d"], lw["d_q_gain"], lw["w_uq"], lw["d_kv_gain"], lw["w_uk"], lw["w_uv"])


def _attn_a_kernel(q_ref, k_ref, v_ref, tab_ref, o_ref, st_ref, *, n):
    nblk = n // QBLK_A
    lane = lax.broadcasted_iota(jnp.int32, (QBLK_A, LANES), 1)

    def body(i, carry):
        q0 = pl.multiple_of(i * QBLK_A, QBLK_A)
        ks = pl.multiple_of(jnp.clip(i * QBLK_A - BAND_HALF, 0, n - 2 * QBLK_A), BAND_HALF)
        var = jnp.where(i == 0, 1, jnp.where(i == nblk - 1, 2, 0))
        st = jnp.zeros((QBLK_A, LANES), F32)
        for hd in range(N_HEADS):
            cols = slice(hd * HEAD_DIM, (hd + 1) * HEAD_DIM)
            q = q_ref[pl.ds(q0, QBLK_A), cols]
            k = k_ref[pl.ds(ks, 2 * QBLK_A), cols]
            v = v_ref[pl.ds(ks, 2 * QBLK_A), cols]
            s = _dot_nt(q, k) + tab_ref[var, hd]
            m = jnp.max(s, axis=-1, keepdims=True)
            e = jnp.exp(s - m)
            l = jnp.sum(e, axis=-1, keepdims=True)
            o_ref[pl.ds(q0, QBLK_A), cols] = _dot(e.astype(BF16), v) / l
            st = jnp.where(lane == hd, m + jnp.log(l), st)
        st_ref[pl.ds(q0, QBLK_A), :] = st
        return carry

    lax.fori_loop(0, nblk, body, 0)


def _attn_a(a_qkv, tab, nb, dil):
    n = SEQ // dil
    view = a_qkv.reshape(nb, n, dil * 3 * GROUP_WIDTH)
    o, st = pl.pallas_call(
        functools.partial(_attn_a_kernel, n=n),
        grid=(nb, dil),
        in_specs=[pl.BlockSpec((None, n, GROUP_WIDTH), lambda b, r: (b, 0, 3 * r)),
                  pl.BlockSpec((None, n, GROUP_WIDTH), lambda b, r: (b, 0, 3 * r + 1)),
                  pl.BlockSpec((None, n, GROUP_WIDTH), lambda b, r: (b, 0, 3 * r + 2)),
                  pl.BlockSpec(tab.shape, lambda b, r: (0, 0, 0, 0))],
        out_specs=[pl.BlockSpec((None, n, GROUP_WIDTH), lambda b, r: (b, 0, r)),
                   pl.BlockSpec((None, n, LANES), lambda b, r: (b, 0, r))],
        out_shape=[jax.ShapeDtypeStruct((nb, n, dil * GROUP_WIDTH), F32),
                   jax.ShapeDtypeStruct((nb, n, dil * LANES), F32)],
        compiler_params=_cparams(2),
        name=f"attn_a_d{dil}",
    )(view, view, view, tab)
    return o.reshape(nb * SEQ, GROUP_WIDTH), st.reshape(nb * SEQ, LANES)


def _flash(q, k_at, v_at, m_scr, l_scr, acc_scr):
    dv = acc_scr.shape[-1]
    reps = TK_DENSE // LANES

    s = _dot_nt(q, k_at(0))
    m0 = jnp.max(s, axis=-1, keepdims=True)
    p = jnp.exp(s - m0)
    m_scr[...] = jnp.broadcast_to(m0, m_scr.shape)
    l_scr[...] = jnp.broadcast_to(jnp.sum(p, axis=-1, keepdims=True), l_scr.shape)
    acc_scr[...] = _dot(p.astype(BF16), v_at(0))

    def body(c, carry):
        s = _dot_nt(q, k_at(c))
        m_prev = m_scr[...]
        m_new = jnp.maximum(m_prev, jnp.max(s, axis=-1, keepdims=True))
        alpha = jnp.exp(m_prev - m_new)
        p = jnp.exp(s - jnp.tile(m_new, (1, reps)))
        l_scr[...] = alpha * l_scr[...] + jnp.sum(p, axis=-1, keepdims=True)
        acc_scr[...] = alpha[:, :dv] * acc_scr[...] + _dot(p.astype(BF16), v_at(c))
        m_scr[...] = m_new
        return carry

    lax.fori_loop(1, SEQ // TK_DENSE, body, 0)
    return acc_scr[...] / l_scr[:, :dv]


def _chunk(c):
    return pl.ds(pl.multiple_of(c * TK_DENSE, TK_DENSE), TK_DENSE)


def _attn_b_kernel(q_ref, kv_ref, o_ref, m_scr, l_scr, acc_scr):
    for g in range(2):
        qa = q_ref[:, g * LANES:g * LANES + HEAD_DIM]
        qb = q_ref[:, g * LANES + HEAD_DIM:(g + 1) * LANES]
        q = jnp.concatenate([qa, qb], axis=0)
        kcols = slice(g * HEAD_DIM, (g + 1) * HEAD_DIM)
        vcols = slice(LANES + g * HEAD_DIM, LANES + (g + 1) * HEAD_DIM)
        o = _flash(q, lambda c: kv_ref[_chunk(c), kcols], lambda c: kv_ref[_chunk(c), vcols],
                   m_scr, l_scr, acc_scr)
        o_ref[:, g * LANES:g * LANES + HEAD_DIM] = o[:TQ_DENSE]
        o_ref[:, g * LANES + HEAD_DIM:(g + 1) * LANES] = o[TQ_DENSE:]


def _attn_b(b_q, b_kv, nb):
    q3 = b_q.reshape(nb, SEQ, GROUP_WIDTH)
    kv3 = b_kv.reshape(nb, SEQ, 2 * LANES)
    o = pl.pallas_call(
        _attn_b_kernel,
        grid=(nb, SEQ // TQ_DENSE),
        in_specs=[pl.BlockSpec((None, TQ_DENSE, GROUP_WIDTH), lambda b, i: (b, i, 0)),
                  pl.BlockSpec((None, SEQ, 2 * LANES), lambda b, i: (b, 0, 0))],
        out_specs=pl.BlockSpec((None, TQ_DENSE, GROUP_WIDTH), lambda b, i: (b, i, 0)),
        out_shape=jax.ShapeDtypeStruct((nb, SEQ, GROUP_WIDTH), F32),
        scratch_shapes=[pltpu.VMEM((2 * TQ_DENSE, LANES), F32), pltpu.VMEM((2 * TQ_DENSE, LANES), F32),
                        pltpu.VMEM((2 * TQ_DENSE, HEAD_DIM), F32)],
        compiler_params=_cparams(2),
        name="attn_b",
    )(q3, kv3)
    return o.reshape(nb * SEQ, GROUP_WIDTH)


def _attn_d_kernel(q_ref, k_ref, v_ref, o_ref, m_scr, l_scr, acc_scr):
    for j in range(2):
        hcols = slice(j * LANES, (j + 1) * LANES)
        vcols = slice(j * D_V, (j + 1) * D_V)
        o = _flash(q_ref[:, hcols], lambda c: k_ref[_chunk(c), hcols], lambda c: v_ref[_chunk(c), vcols],
                   m_scr, l_scr, acc_scr)
        o_ref[:, vcols] = o


def _attn_d(d_q, d_k, d_v, nb):
    q3 = d_q.reshape(nb, SEQ, N_HEADS * LANES)
    k3 = d_k.reshape(nb, SEQ, N_HEADS * LANES)
    v3 = d_v.reshape(nb, SEQ, GROUP_WIDTH)
    o = pl.pallas_call(
        _attn_d_kernel,
        grid=(nb, 2, SEQ // TQ_DENSE),
        in_specs=[pl.BlockSpec((None, TQ_DENSE, 2 * LANES), lambda b, p, i: (b, i, p)),
                  pl.BlockSpec((None, SEQ, 2 * LANES), lambda b, p, i: (b, 0, p)),
                  pl.BlockSpec((None, SEQ, LANES), lambda b, p, i: (b, 0, p))],
        out_specs=pl.BlockSpec((None, TQ_DENSE, LANES), lambda b, p, i: (b, i, p)),
        out_shape=jax.ShapeDtypeStruct((nb, SEQ, GROUP_WIDTH), F32),
        scratch_shapes=[pltpu.VMEM((TQ_DENSE, LANES), F32), pltpu.VMEM((TQ_DENSE, LANES), F32),
                        pltpu.VMEM((TQ_DENSE, D_V), F32)],
        compiler_params=_cparams(3),
        name="attn_d",
    )(q3, k3, v3)
    return o.reshape(nb * SEQ, GROUP_WIDTH)


def _attn_c_kernel(q_ref, k_ref, v_ref, tab_ref, o_ref):
    i = pl.program_id(1)
    nkeys = NA_ROWS * GRID_W
    for rr in range(ROWS_C):
        r = i * ROWS_C + rr
        rs = jnp.clip(r - NA_ROWS // 2, 0, GRID_ROWS - NA_ROWS)
        var = rs - r + NA_ROWS - 1
        k0 = pl.multiple_of(rs * GRID_W, GRID_W)
        rows = slice(rr * GRID_W, (rr + 1) * GRID_W)
        for hd in range(N_HEADS):
            cols = slice(hd * HEAD_DIM, (hd + 1) * HEAD_DIM)
            s = _dot_nt(q_ref[rows, cols], k_ref[pl.ds(k0, nkeys), cols]) + tab_ref[var, hd]
            m = jnp.max(s, axis=-1, keepdims=True)
            e = jnp.exp(s - m)
            l = jnp.sum(e, axis=-1, keepdims=True)
            o_ref[rows, cols] = _dot(e.astype(BF16), v_ref[pl.ds(k0, nkeys), cols]) / l


def _attn_c(c_qkv, tab, nb):
    view = c_qkv.reshape(nb, SEQ, 3 * GROUP_WIDTH)
    tq = ROWS_C * GRID_W
    o = pl.pallas_call(
        _attn_c_kernel,
        grid=(nb, SEQ // tq),
        in_specs=[pl.BlockSpec((None, tq, GROUP_WIDTH), lambda b, i: (b, i, 0)),
                  pl.BlockSpec((None, SEQ, GROUP_WIDTH), lambda b, i: (b, 0, 1)),
                  pl.BlockSpec((None, SEQ, GROUP_WIDTH), lambda b, i: (b, 0, 2)),
                  pl.BlockSpec(tab.shape, lambda b, i: (0, 0, 0, 0))],
        out_specs=pl.BlockSpec((None, tq, GROUP_WIDTH), lambda b, i: (b, i, 0)),
        out_shape=jax.ShapeDtypeStruct((nb, SEQ, GROUP_WIDTH), F32),
        compiler_params=_cparams(2),
        name="attn_c",
    )(view, view, view, tab)
    return o.reshape(nb * SEQ, GROUP_WIDTH)


def _out_ffn_kernel(x_ref, oa1_ref, sa1_ref, oa2_ref, sa2_ref, oa3_ref, sa3_ref, ob_ref, oc_ref, od_ref,
                    og_ref, wout_ref, nf_ref, wg_ref, wu_ref, wd_ref, fin_ref, y_ref, *, final):
    lses = [sa1_ref[...], sa2_ref[...], sa3_ref[...]]
    mx = jnp.maximum(jnp.maximum(lses[0], lses[1]), lses[2])
    es = [jnp.exp(s - mx) for s in lses]
    den = es[0] + es[1] + es[2]
    head_of_lane = lax.broadcasted_iota(jnp.int32, (TM_FFN, GROUP_WIDTH), 1) // HEAD_DIM

    def per_head(w):
        out = jnp.broadcast_to(w[:, 0:1], (TM_FFN, GROUP_WIDTH))
        for hd in range(1, N_HEADS):
            out = jnp.where(head_of_lane == hd, w[:, hd:hd + 1], out)
        return out

    o_a = (per_head(es[0] / den) * oa1_ref[...] + per_head(es[1] / den) * oa2_ref[...]
           + per_head(es[2] / den) * oa3_ref[...])

    x = x_ref[...]
    attn = jnp.zeros((TM_FFN, D_MODEL), F32)
    for g, o in enumerate((o_a, ob_ref[...], oc_ref[...], od_ref[...])):
        cols = slice(g * GROUP_WIDTH, (g + 1) * GROUP_WIDTH)
        attn = attn + _dot(_rms(o, og_ref[:, cols]).astype(BF16), wout_ref[cols, :])
    x1 = x + attn

    h = _rms(x1, nf_ref[...]).astype(BF16)
    ff = jnp.zeros((TM_FFN, D_MODEL), F32)
    for c in range(D_FF // FF_CHUNK):
        cols = slice(c * FF_CHUNK, (c + 1) * FF_CHUNK)
        gate = _dot(h, wg_ref[:, cols])
        up = _dot(h, wu_ref[:, cols])
        act = gate * (1.0 / (1.0 + jnp.exp(-gate))) * up
        ff = ff + _dot(act.astype(BF16), wd_ref[cols, :])
    x2 = x1 + ff
    if final:
        x2 = _rms(x2, fin_ref[...])
    y_ref[...] = x2


def _out_ffn(x2d, attn_outs, lw, final_norm, final):
    t = x2d.shape[0]
    row = lambda i: (i, 0)

    def resident(a):
        return pl.BlockSpec(a.shape, lambda i: (0, 0), pipeline_mode=pl.Buffered(1))

    (oa1, sa1), (oa2, sa2), (oa3, sa3), ob, oc, od = attn_outs
    wide = pl.BlockSpec((TM_FFN, GROUP_WIDTH), row)
    stat = pl.BlockSpec((TM_FFN, LANES), row)
    weights = (lw["out_gain"], lw["w_out"], lw["norm_ffn"], lw["w_gate"], lw["w_up"], lw["w_down"], final_norm)
    return pl.pallas_call(
        functools.partial(_out_ffn_kernel, final=final),
        grid=(t // TM_FFN,),
        in_specs=[pl.BlockSpec((TM_FFN, D_MODEL), row), wide, stat, wide, stat, wide, stat, wide, wide, wide]
                 + [resident(w) for w in weights],
        out_specs=pl.BlockSpec((TM_FFN, D_MODEL), row),
        out_shape=jax.ShapeDtypeStruct((t, D_MODEL), F32),
        compiler_params=_cparams(1),
        name="out_ffn",
    )(x2d, oa1, sa1, oa2, sa2, oa3, sa3, ob, oc, od, *weights)


def _rope_angles(pos, dim):
    inv = 1.0 / (ROPE_THETA ** (jnp.arange(0, dim, 2, dtype=F32) / dim))
    return pos.astype(F32)[:, None] * inv[None, :]


def _t5_bucket(rel):
    nb = T5_BUCKETS // 2
    max_exact = nb // 2
    n = jnp.abs(rel)
    n_f = jnp.maximum(n, max_exact).astype(F32)
    large = max_exact + (jnp.log(n_f / max_exact) / math.log(T5_MAX_DIST / max_exact)
                         * (nb - max_exact)).astype(jnp.int32)
    large = jnp.minimum(large, nb - 1)
    return jnp.where(rel > 0, nb, 0) + jnp.where(n < max_exact, n, large)


def _tables(t5_bias):
    t = jnp.arange(SEQ, dtype=jnp.int32)
    z16 = jnp.zeros((SEQ, 16), F32)

    def trio(parts):
        return jnp.stack([jnp.concatenate(p, axis=1) for p in parts])

    ang_r = _rope_angles(t // GRID_W, HEAD_DIM // 2)
    ang_c = _rope_angles(t % GRID_W, HEAD_DIM // 2)
    cr, sr, cc, sc = jnp.cos(ang_r), jnp.sin(ang_r), jnp.cos(ang_c), jnp.sin(ang_c)
    rope_b = trio(([cr, cr, cc, cc] * 2, [-sr, z16, -sc, z16] * 2, [z16, sr, z16, sc] * 2))

    ang_t = _rope_angles(t, D_ROPE)
    ct, st = jnp.cos(ang_t), jnp.sin(ang_t)
    ones64, z64, z32 = jnp.ones((SEQ, 64), F32), jnp.zeros((SEQ, 64), F32), jnp.zeros((SEQ, 32), F32)
    rope_d = trio(([ones64, ct, ct, z32], [z64, -st, z16, z32], [z64, z16, st, z32]))

    seg = jnp.kron(jnp.eye(2, dtype=F32), jnp.full((HEAD_DIM, HEAD_DIM), 1.0 / HEAD_DIM, F32)).astype(BF16)

    qi = jnp.arange(QBLK_A, dtype=jnp.int32)[:, None]
    kj = jnp.arange(2 * QBLK_A, dtype=jnp.int32)[None, :]
    band = []
    for _, dil in DILATED_PATTERNS:
        off = jnp.arange(-BAND_HALF, BAND_HALF + 1, dtype=jnp.int32) * dil
        bias = t5_bias[_t5_bucket(off)].T.astype(F32)
        variants = []
        for shift in (-BAND_HALF, 0, -2 * BAND_HALF):
            delta = kj - qi + shift
            inside = jnp.abs(delta) <= BAND_HALF
            vals = bias[:, jnp.clip(delta + BAND_HALF, 0, 2 * BAND_HALF)]
            variants.append(jnp.where(inside[None], vals, NEG_INF))
        band.append(jnp.stack(variants))
    return {"rope_b": rope_b, "rope_d": rope_d, "seg": seg, "band": band}


def _na_table(rpb):
    cols = jnp.arange(GRID_W, dtype=jnp.int32)
    cs = jnp.clip(cols - NA_COLS // 2, 0, GRID_W - NA_COLS)
    kc = jnp.arange(GRID_W, dtype=jnp.int32)
    inside = (kc[None, :] >= cs[:, None]) & (kc[None, :] < cs[:, None] + NA_COLS)
    dc = jnp.clip(kc[None, :] - cols[:, None] + NA_COLS - 1, 0, 2 * NA_COLS - 2)
    var = jnp.arange(NA_ROWS, dtype=jnp.int32)
    kr = jnp.arange(NA_ROWS, dtype=jnp.int32)
    dr = var[:, None] + kr[None, :]
    vals = rpb[:, dr[:, :, None, None], dc[None, None, :, :]]
    vals = jnp.where(inside[None, None, None], vals.astype(F32), NEG_INF)
    return jnp.transpose(vals, (1, 0, 3, 2, 4)).reshape(NA_ROWS, N_HEADS, GRID_W, NA_ROWS * GRID_W)


def _layer_weights(l, norm_mix, w_in, b_q_gain, b_k_gain, d_q_gain, d_w_uq, d_kv_gain, d_w_ukv,
                   out_gain, w_out, norm_ffn, w_gate, w_up, w_down):
    d_in = w_in.shape[-1]
    kr_lo = d_in - D_ROPE
    w_in_p = jnp.zeros((D_MODEL, D_IN_PAD), F32)
    w_in_p = w_in_p.at[:, :kr_lo].set(w_in[l][:, :kr_lo])
    w_in_p = w_in_p.at[:, kr_lo + D_NOPE:kr_lo + D_NOPE + D_ROPE].set(w_in[l][:, kr_lo:])
    dqk = D_NOPE + D_ROPE
    w_uq = jnp.zeros((D_Q_LORA, N_HEADS, LANES), F32).at[:, :, :dqk].set(
        d_w_uq[l].reshape(D_Q_LORA, N_HEADS, dqk)).reshape(D_Q_LORA, N_HEADS * LANES)
    ukv = d_w_ukv[l].reshape(D_KV_LORA, N_HEADS, D_NOPE + D_V)
    w_uk = jnp.zeros((D_KV_LORA, N_HEADS, LANES), F32).at[:, :, :D_NOPE].set(
        ukv[:, :, :D_NOPE]).reshape(D_KV_LORA, N_HEADS * LANES)
    w_uv = ukv[:, :, D_NOPE:].reshape(D_KV_LORA, N_HEADS * D_V)
    return {
        "norm_mix": norm_mix[l][None, :], "w_in": w_in_p.astype(BF16),
        "b_q_gain": jnp.tile(b_q_gain[l], 2)[None, :], "b_k_gain": jnp.tile(b_k_gain[l], 2)[None, :],
        "d_q_gain": d_q_gain[l][None, :], "w_uq": w_uq.astype(BF16),
        "d_kv_gain": d_kv_gain[l][None, :], "w_uk": w_uk.astype(BF16), "w_uv": w_uv.astype(BF16),
        "out_gain": out_gain[l][None, :], "w_out": w_out[l].astype(BF16), "norm_ffn": norm_ffn[l][None, :],
        "w_gate": w_gate[l].astype(BF16), "w_up": w_up[l].astype(BF16), "w_down": w_down[l].astype(BF16),
    }


def _trunk(x, layers, tabs, na_tabs, final_norm):
    nb = x.shape[0]
    x2d = x.reshape(nb * SEQ, D_MODEL)
    for l, lw in enumerate(layers):
        a_qkv, b_q, b_kv, c_qkv, d_q, d_k, d_v = _proj(x2d, lw, tabs)
        outs_a = tuple(_attn_a(a_qkv, tabs["band"][p], nb, dil)
                       for p, (_, dil) in enumerate(DILATED_PATTERNS))
        o_b = _attn_b(b_q, b_kv, nb)
        o_c = _attn_c(c_qkv, na_tabs[l], nb)
        o_d = _attn_d(d_q, d_k, d_v, nb)
        x2d = _out_ffn(x2d, outs_a + (o_b, o_c, o_d), lw, final_norm, final=(l == DEPTH - 1))
    return x2d.reshape(nb, SEQ, D_MODEL)


def kernel(x_prompt, x_sample, t5_bias, norm_mix, w_in, b_q_gain, b_k_gain, c_rpb, d_q_gain, d_w_uq,
           d_kv_gain, d_w_ukv, out_gain, w_out, norm_ffn, w_gate, w_up, w_down, final_norm):
    tabs = _tables(t5_bias)
    na_tabs = [_na_table(c_rpb[l]) for l in range(DEPTH)]
    layers = [_layer_weights(l, norm_mix, w_in, b_q_gain, b_k_gain, d_q_gain, d_w_uq, d_kv_gain, d_w_ukv,
                             out_gain, w_out, norm_ffn, w_gate, w_up, w_down) for l in range(DEPTH)]
    fin = final_norm[None, :]
    return (_trunk(x_prompt, layers, tabs, na_tabs, fin), _trunk(x_sample, layers, tabs, na_tabs, fin))
```

```python
import functools
import math

import jax
import jax.numpy as jnp
import numpy as np
from jax import lax
from jax.experimental import pallas as pl
from jax.experimental.pallas import tpu as pltpu

F32 = jnp.float32
BF16 = jnp.bfloat16

D_MODEL = 1024
SEQ = 4096
DEPTH = 2
HEAD_DIM = 64
N_HEADS = 4
GROUP_WIDTH = N_HEADS * HEAD_DIM
DILATED_PATTERNS = ((128, 1), (512, 4), (2048, 16))
BAND_HALF = 64
T5_BUCKETS = 32
T5_MAX_DIST = 1024
GRID_W = 64
GRID_ROWS = SEQ // GRID_W
NA_ROWS = 8
NA_COLS = 16
D_Q_LORA = 256
D_KV_LORA = 128
D_NOPE = 64
D_ROPE = 32
D_V = 64
ROPE_THETA = 10000.0
D_FF = 2816
RMS_EPS = 1e-6
NEG_INF = -1e30

LANES = 128
D_IN_PAD = 2560
COL_A, COL_B, COL_C, COL_D = 0, 768, 1280, 2048
ATTN_SCALE = HEAD_DIM ** -0.5
MLA_SCALE = (D_NOPE + D_ROPE) ** -0.5
LOG2E = math.log2(math.e)

TM_PROJ = 512
QBLK_A = 128
TQ_DENSE = 256
TK_DENSE = 512
ONES_ROWS = 16
ROWS_C = 4
TM_FFN = 256
FF_CHUNK = 1408
VMEM_LIMIT = 56 * 1024 * 1024


def _cparams(n_axes):
    return pltpu.CompilerParams(dimension_semantics=("arbitrary",) * n_axes,
                                vmem_limit_bytes=VMEM_LIMIT)


def _rms(x, g):
    return x * lax.rsqrt(jnp.mean(x * x, axis=-1, keepdims=True) + RMS_EPS) * g


def _dot(a, b):
    return jnp.dot(a, b, preferred_element_type=F32)


def _dot_nt(a, b):
    return lax.dot_general(a, b, (((1,), (1,)), ((), ())), preferred_element_type=F32)


def _rope(y, tab):
    return y * tab[0] + pltpu.roll(y, LANES - 16, 1) * tab[1] + pltpu.roll(y, 16, 1) * tab[2]


def _proj_kernel(x_ref, gmix_ref, w_ref, bqg_ref, bkg_ref, seg_ref, tabb_ref, tabd_ref,
                 dqg_ref, wuq_ref, dkvg_ref, wuk_ref, wuv_ref,
                 a_ref, bqt_ref, bk_ref, bvt_ref, c_ref, dqt_ref, dk_ref, dvt_ref):
    h = _rms(x_ref[...], gmix_ref[...]).astype(BF16)

    def proj(lo, hi):
        return _dot(h, w_ref[:, lo:hi])

    pa = proj(COL_A, COL_B)
    a_ref[:, :GROUP_WIDTH] = (pa[:, :GROUP_WIDTH] * ATTN_SCALE).astype(BF16)
    a_ref[:, GROUP_WIDTH:] = pa[:, GROUP_WIDTH:].astype(BF16)

    pc = proj(COL_C, COL_D)
    c_ref[:, :GROUP_WIDTH] = (pc[:, :GROUP_WIDTH] * ATTN_SCALE).astype(BF16)
    c_ref[:, GROUP_WIDTH:] = pc[:, GROUP_WIDTH:].astype(BF16)

    pb = proj(COL_B, COL_C)
    seg = seg_ref[...]
    tabb = tabb_ref[...]

    def head_norm_rope(chunk, gain):
        sq = chunk * chunk
        hi = sq.astype(BF16)
        lo = (sq - hi.astype(F32)).astype(BF16)
        ms = _dot(hi, seg) + _dot(lo, seg)
        return _rope(chunk * lax.rsqrt(ms + RMS_EPS) * gain, tabb)

    zeros_half = jnp.zeros((HEAD_DIM, TM_PROJ), BF16)
    for c in range(2):
        q = head_norm_rope(pb[:, c * LANES:(c + 1) * LANES], bqg_ref[...])
        qt = (q * (ATTN_SCALE * LOG2E)).T.astype(BF16)
        for j in range(2):
            r0 = (2 * c + j) * LANES
            bqt_ref[r0:r0 + HEAD_DIM, :] = qt[j * HEAD_DIM:(j + 1) * HEAD_DIM]
            bqt_ref[r0 + HEAD_DIM:r0 + LANES, :] = zeros_half
    kb = head_norm_rope(pb[:, 2 * LANES:3 * LANES], bkg_ref[...])
    bk_ref[:, :LANES] = kb.astype(BF16)
    bk_ref[:, LANES:] = pltpu.roll(kb, HEAD_DIM, 1).astype(BF16)
    bvt_ref[...] = pb[:, 3 * LANES:].T.astype(BF16)

    pd = proj(COL_D, D_IN_PAD)
    tabd = tabd_ref[...]
    dq = _rms(pd[:, :D_Q_LORA], dqg_ref[...]).astype(BF16)
    qd = _dot(dq, wuq_ref[...])
    dkv = _rms(pd[:, D_Q_LORA:D_Q_LORA + D_KV_LORA], dkvg_ref[...]).astype(BF16)
    kn = _dot(dkv, wuk_ref[...])
    dvt_ref[...] = _dot(dkv, wuv_ref[...]).T.astype(BF16)
    kr = _rope(pd[:, D_Q_LORA + D_KV_LORA:], tabd)
    for hd in range(N_HEADS):
        sl = slice(hd * LANES, (hd + 1) * LANES)
        dqt_ref[sl, :] = (_rope(qd[:, sl], tabd) * (MLA_SCALE * LOG2E)).T.astype(BF16)
        dk_ref[:, sl] = (kn[:, sl] + kr).astype(BF16)


def _proj(x2d, lw, tabs):
    t = x2d.shape[0]
    blocks_per_seq = SEQ // TM_PROJ
    row = lambda i: (i, 0)
    const2 = lambda i: (0, 0)
    pos3 = lambda i: (0, i % blocks_per_seq, 0)

    def full(a):
        return pl.BlockSpec(a.shape, const2)

    col = lambda i: (0, i)
    outs = ((768, False), (512, True), (256, False), (128, True), (768, False), (512, True), (512, False),
            (256, True))
    return pl.pallas_call(
        _proj_kernel,
        grid=(t // TM_PROJ,),
        in_specs=[pl.BlockSpec((TM_PROJ, D_MODEL), row), full(lw["norm_mix"]), full(lw["w_in"]),
                  full(lw["b_q_gain"]), full(lw["b_k_gain"]), full(tabs["seg"]),
                  pl.BlockSpec((3, TM_PROJ, LANES), pos3), pl.BlockSpec((3, TM_PROJ, LANES), pos3),
                  full(lw["d_q_gain"]), full(lw["w_uq"]), full(lw["d_kv_gain"]), full(lw["w_uk"]),
                  full(lw["w_uv"])],
        out_specs=[pl.BlockSpec((w, TM_PROJ), col) if tr else pl.BlockSpec((TM_PROJ, w), row) for w, tr in outs],
        out_shape=[jax.ShapeDtypeStruct((w, t) if tr else (t, w), BF16) for w, tr in outs],
        compiler_params=_cparams(1),
        name="proj",
    )(x2d, lw["norm_mix"], lw["w_in"], lw["b_q_gain"], lw["b_k_gain"], tabs["seg"], tabs["rope_b"],
      tabs["rope_d"], lw["d_q_gain"], lw["w_uq"], lw["d_kv_gain"], lw["w_uk"], lw["w_uv"])


def _attn_a_kernel(q_ref, k_ref, v_ref, tab_ref, o_ref, st_ref, *, n):
    nblk = n // QBLK_A
    lane = lax.broadcasted_iota(jnp.int32, (QBLK_A, LANES), 1)

    def body(i, carry):
        q0 = pl.multiple_of(i * QBLK_A, QBLK_A)
        ks = pl.multiple_of(jnp.clip(i * QBLK_A - BAND_HALF, 0, n - 2 * QBLK_A), BAND_HALF)
        var = jnp.where(i == 0, 1, jnp.where(i == nblk - 1, 2, 0))
        st = jnp.zeros((QBLK_A, LANES), F32)
        for hd in range(N_HEADS):
            cols = slice(hd * HEAD_DIM, (hd + 1) * HEAD_DIM)
            q = q_ref[pl.ds(q0, QBLK_A), cols]
            k = k_ref[pl.ds(ks, 2 * QBLK_A), cols]
            v = v_ref[pl.ds(ks, 2 * QBLK_A), cols]
            s = _dot_nt(q, k) + tab_ref[var, hd]
            m = jnp.max(s, axis=-1, keepdims=True)
            e = jnp.exp(s - m)
            l = jnp.sum(e, axis=-1, keepdims=True)
            o_ref[pl.ds(q0, QBLK_A), cols] = _dot(e.astype(BF16), v) / l
            st = jnp.where(lane == hd, m + jnp.log(l), st)
        st_ref[pl.ds(q0, QBLK_A), :] = st
        return carry

    lax.fori_loop(0, nblk, body, 0)


def _attn_a(a_qkv, tab, nb, dil):
    n = SEQ // dil
    view = a_qkv.reshape(nb, n, dil * 3 * GROUP_WIDTH)
    o, st = pl.pallas_call(
        functools.partial(_attn_a_kernel, n=n),
        grid=(nb, dil),
        in_specs=[pl.BlockSpec((None, n, GROUP_WIDTH), lambda b, r: (b, 0, 3 * r)),
                  pl.BlockSpec((None, n, GROUP_WIDTH), lambda b, r: (b, 0, 3 * r + 1)),
                  pl.BlockSpec((None, n, GROUP_WIDTH), lambda b, r: (b, 0, 3 * r + 2)),
                  pl.BlockSpec(tab.shape, lambda b, r: (0, 0, 0, 0))],
        out_specs=[pl.BlockSpec((None, n, GROUP_WIDTH), lambda b, r: (b, 0, r)),
                   pl.BlockSpec((None, n, LANES), lambda b, r: (b, 0, r))],
        out_shape=[jax.ShapeDtypeStruct((nb, n, dil * GROUP_WIDTH), F32),
                   jax.ShapeDtypeStruct((nb, n, dil * LANES), F32)],
        compiler_params=_cparams(2),
        name=f"attn_a_d{dil}",
    )(view, view, view, tab)
    return o.reshape(nb * SEQ, GROUP_WIDTH), st.reshape(nb * SEQ, LANES)


def _flash(problems):
    n_chunks = SEQ // TK_DENSE
    ones = jnp.ones((ONES_ROWS, TK_DENSE), BF16)
    state = [None] * len(problems)
    s_cur = [_dot(k_at(0), q_t) for q_t, k_at, _ in problems]
    for c in range(n_chunks):
        ps, ms = [], []
        for i, s in enumerate(s_cur):
            col_max = jnp.max(s, axis=0, keepdims=True)
            m_new = col_max if c == 0 else jnp.maximum(state[i][0], col_max)
            ms.append(m_new)
            ps.append(jnp.exp2(s - m_new).astype(BF16))
        if c + 1 < n_chunks:
            s_next = [_dot(k_at(c + 1), q_t) for q_t, k_at, _ in problems]
        for i, (p, m_new, (_, _, v_t_at)) in enumerate(zip(ps, ms, problems)):
            v_ext = jnp.concatenate([v_t_at(c), ones], axis=0)
            if c == 0:
                acc = _dot(v_ext, p)
            else:
                m_prev, acc_prev = state[i]
                acc = jnp.exp2(m_prev - m_new) * acc_prev + _dot(v_ext, p)
            state[i] = (m_new, acc)
        s_cur = s_next
    outs = []
    for (_, _, v_t_at), (_, acc) in zip(problems, state):
        dv = acc.shape[0] - ONES_ROWS
        outs.append(acc[:dv] / acc[dv:dv + 1])
    return outs


def _chunk(c):
    return pl.ds(c * TK_DENSE, TK_DENSE)


def _attn_b_kernel(qt_ref, k_ref, vt_ref, o_ref):
    problems = []
    for g in range(2):
        q_t = jnp.concatenate([qt_ref[(2 * g) * LANES:(2 * g + 1) * LANES, :],
                               qt_ref[(2 * g + 1) * LANES:(2 * g + 2) * LANES, :]], axis=1)
        problems.append((q_t,
                         lambda c, g=g: k_ref[_chunk(c), g * LANES:(g + 1) * LANES],
                         lambda c, g=g: vt_ref[g * HEAD_DIM:(g + 1) * HEAD_DIM, _chunk(c)]))
    outs = _flash(problems)
    o_t = jnp.concatenate([o[:, j * TQ_DENSE:(j + 1) * TQ_DENSE] for o in outs for j in range(2)], axis=0)
    o_ref[...] = o_t.T


def _attn_b(b_qt, b_k, b_vt, nb):
    nq = SEQ // TQ_DENSE
    o = pl.pallas_call(
        _attn_b_kernel,
        grid=(nb, nq),
        in_specs=[pl.BlockSpec((N_HEADS * LANES, TQ_DENSE), lambda b, i: (0, b * nq + i)),
                  pl.BlockSpec((None, SEQ, 2 * LANES), lambda b, i: (b, 0, 0)),
                  pl.BlockSpec((2 * HEAD_DIM, SEQ), lambda b, i: (0, b))],
        out_specs=pl.BlockSpec((None, TQ_DENSE, GROUP_WIDTH), lambda b, i: (b, i, 0)),
        out_shape=jax.ShapeDtypeStruct((nb, SEQ, GROUP_WIDTH), F32),
        compiler_params=_cparams(2),
        name="attn_b",
    )(b_qt, b_k.reshape(nb, SEQ, 2 * LANES), b_vt)
    return o.reshape(nb * SEQ, GROUP_WIDTH)


def _attn_d_kernel(qt_ref, k_ref, vt_ref, o_ref):
    problems = []
    for j in range(2):
        problems.append((qt_ref[j * LANES:(j + 1) * LANES, :],
                         lambda c, j=j: k_ref[_chunk(c), j * LANES:(j + 1) * LANES],
                         lambda c, j=j: vt_ref[j * D_V:(j + 1) * D_V, _chunk(c)]))
    o_ref[...] = jnp.concatenate(_flash(problems), axis=0).T


def _attn_d(d_qt, d_k, d_vt, nb):
    nq = SEQ // TQ_DENSE
    o = pl.pallas_call(
        _attn_d_kernel,
        grid=(nb, 2, nq),
        in_specs=[pl.BlockSpec((2 * LANES, TQ_DENSE), lambda b, p, i: (p, b * nq + i)),
                  pl.BlockSpec((None, SEQ, 2 * LANES), lambda b, p, i: (b, 0, p)),
                  pl.BlockSpec((2 * D_V, SEQ), lambda b, p, i: (p, b))],
        out_specs=pl.BlockSpec((None, TQ_DENSE, LANES), lambda b, p, i: (b, i, p)),
        out_shape=jax.ShapeDtypeStruct((nb, SEQ, GROUP_WIDTH), F32),
        compiler_params=_cparams(3),
        name="attn_d",
    )(d_qt, d_k.reshape(nb, SEQ, N_HEADS * LANES), d_vt)
    return o.reshape(nb * SEQ, GROUP_WIDTH)


def _attn_c_kernel(q_ref, k_ref, v_ref, tab_ref, o_ref):
    i = pl.program_id(1)
    nkeys = NA_ROWS * GRID_W
    for rr in range(ROWS_C):
        r = i * ROWS_C + rr
        rs = jnp.clip(r - NA_ROWS // 2, 0, GRID_ROWS - NA_ROWS)
        var = rs - r + NA_ROWS - 1
        k0 = pl.multiple_of(rs * GRID_W, GRID_W)
        rows = slice(rr * GRID_W, (rr + 1) * GRID_W)
        for hd in range(N_HEADS):
            cols = slice(hd * HEAD_DIM, (hd + 1) * HEAD_DIM)
            s = _dot_nt(q_ref[rows, cols], k_ref[pl.ds(k0, nkeys), cols]) + tab_ref[var, hd]
            m = jnp.max(s, axis=-1, keepdims=True)
            e = jnp.exp(s - m)
            l = jnp.sum(e, axis=-1, keepdims=True)
            o_ref[rows, cols] = _dot(e.astype(BF16), v_ref[pl.ds(k0, nkeys), cols]) / l


def _attn_c(c_qkv, tab, nb):
    view = c_qkv.reshape(nb, SEQ, 3 * GROUP_WIDTH)
    tq = ROWS_C * GRID_W
    o = pl.pallas_call(
        _attn_c_kernel,
        grid=(nb, SEQ // tq),
        in_specs=[pl.BlockSpec((None, tq, GROUP_WIDTH), lambda b, i: (b, i, 0)),
                  pl.BlockSpec((None, SEQ, GROUP_WIDTH), lambda b, i: (b, 0, 1)),
                  pl.BlockSpec((None, SEQ, GROUP_WIDTH), lambda b, i: (b, 0, 2)),
                  pl.BlockSpec(tab.shape, lambda b, i: (0, 0, 0, 0))],
        out_specs=pl.BlockSpec((None, tq, GROUP_WIDTH), lambda b, i: (b, i, 0)),
        out_shape=jax.ShapeDtypeStruct((nb, SEQ, GROUP_WIDTH), F32),
        compiler_params=_cparams(2),
        name="attn_c",
    )(view, view, view, tab)
    return o.reshape(nb * SEQ, GROUP_WIDTH)


def _out_ffn_kernel(x_ref, oa1_ref, sa1_ref, oa2_ref, sa2_ref, oa3_ref, sa3_ref, ob_ref, oc_ref, od_ref,
                    og_ref, wout_ref, nf_ref, wg_ref, wu_ref, wd_ref, fin_ref, y_ref, *, final):
    lses = [sa1_ref[...], sa2_ref[...], sa3_ref[...]]
    mx = jnp.maximum(jnp.maximum(lses[0], lses[1]), lses[2])
    es = [jnp.exp(s - mx) for s in lses]
    den = es[0] + es[1] + es[2]
    head_of_lane = lax.broadcasted_iota(jnp.int32, (TM_FFN, GROUP_WIDTH), 1) // HEAD_DIM

    def per_head(w):
        out = jnp.broadcast_to(w[:, 0:1], (TM_FFN, GROUP_WIDTH))
        for hd in range(1, N_HEADS):
            out = jnp.where(head_of_lane == hd, w[:, hd:hd + 1], out)
        return out

    o_a = (per_head(es[0] / den) * oa1_ref[...] + per_head(es[1] / den) * oa2_ref[...]
           + per_head(es[2] / den) * oa3_ref[...])

    x = x_ref[...]
    attn = jnp.zeros((TM_FFN, D_MODEL), F32)
    for g, o in enumerate((o_a, ob_ref[...], oc_ref[...], od_ref[...])):
        cols = slice(g * GROUP_WIDTH, (g + 1) * GROUP_WIDTH)
        attn = attn + _dot(_rms(o, og_ref[:, cols]).astype(BF16), wout_ref[cols, :])
    x1 = x + attn

    h = _rms(x1, nf_ref[...]).astype(BF16)
    ff = jnp.zeros((TM_FFN, D_MODEL), F32)
    for c in range(D_FF // FF_CHUNK):
        cols = slice(c * FF_CHUNK, (c + 1) * FF_CHUNK)
        gate = _dot(h, wg_ref[:, cols])
        up = _dot(h, wu_ref[:, cols])
        act = gate * (1.0 / (1.0 + jnp.exp(-gate))) * up
        ff = ff + _dot(act.astype(BF16), wd_ref[cols, :])
    x2 = x1 + ff
    if final:
        x2 = _rms(x2, fin_ref[...])
    y_ref[...] = x2


def _out_ffn(x2d, attn_outs, lw, final_norm, final):
    t = x2d.shape[0]
    row = lambda i: (i, 0)

    def resident(a):
        return pl.BlockSpec(a.shape, lambda i: (0, 0), pipeline_mode=pl.Buffered(1))

    (oa1, sa1), (oa2, sa2), (oa3, sa3), ob, oc, od = attn_outs
    wide = pl.BlockSpec((TM_FFN, GROUP_WIDTH), row)
    stat = pl.BlockSpec((TM_FFN, LANES), row)
    weights = (lw["out_gain"], lw["w_out"], lw["norm_ffn"], lw["w_gate"], lw["w_up"], lw["w_down"], final_norm)
    return pl.pallas_call(
        functools.partial(_out_ffn_kernel, final=final),
        grid=(t // TM_FFN,),
        in_specs=[pl.BlockSpec((TM_FFN, D_MODEL), row), wide, stat, wide, stat, wide, stat, wide, wide, wide]
                 + [resident(w) for w in weights],
        out_specs=pl.BlockSpec((TM_FFN, D_MODEL), row),
        out_shape=jax.ShapeDtypeStruct((t, D_MODEL), F32),
        compiler_params=_cparams(1),
        name="out_ffn",
    )(x2d, oa1, sa1, oa2, sa2, oa3, sa3, ob, oc, od, *weights)


def _rope_angles(pos, dim):
    inv = 1.0 / (ROPE_THETA ** (jnp.arange(0, dim, 2, dtype=F32) / dim))
    return pos.astype(F32)[:, None] * inv[None, :]


def _t5_bucket(rel):
    nb = T5_BUCKETS // 2
    max_exact = nb // 2
    n = jnp.abs(rel)
    n_f = jnp.maximum(n, max_exact).astype(F32)
    large = max_exact + (jnp.log(n_f / max_exact) / math.log(T5_MAX_DIST / max_exact)
                         * (nb - max_exact)).astype(jnp.int32)
    large = jnp.minimum(large, nb - 1)
    return jnp.where(rel > 0, nb, 0) + jnp.where(n < max_exact, n, large)


def _tables(t5_bias):
    t = jnp.arange(SEQ, dtype=jnp.int32)
    z16 = jnp.zeros((SEQ, 16), F32)

    def trio(parts):
        return jnp.stack([jnp.concatenate(p, axis=1) for p in parts])

    ang_r = _rope_angles(t // GRID_W, HEAD_DIM // 2)
    ang_c = _rope_angles(t % GRID_W, HEAD_DIM // 2)
    cr, sr, cc, sc = jnp.cos(ang_r), jnp.sin(ang_r), jnp.cos(ang_c), jnp.sin(ang_c)
    rope_b = trio(([cr, cr, cc, cc] * 2, [-sr, z16, -sc, z16] * 2, [z16, sr, z16, sc] * 2))

    ang_t = _rope_angles(t, D_ROPE)
    ct, st = jnp.cos(ang_t), jnp.sin(ang_t)
    ones64, z64, z32 = jnp.ones((SEQ, 64), F32), jnp.zeros((SEQ, 64), F32), jnp.zeros((SEQ, 32), F32)
    rope_d = trio(([ones64, ct, ct, z32], [z64, -st, z16, z32], [z64, z16, st, z32]))

    seg = jnp.kron(jnp.eye(2, dtype=F32), jnp.full((HEAD_DIM, HEAD_DIM), 1.0 / HEAD_DIM, F32)).astype(BF16)

    period = 4 * QBLK_A
    u = np.arange(period)
    rel = np.where(u < period - QBLK_A, u, u - period)
    band = []
    for _, dil in DILATED_PATTERNS:
        off = jnp.arange(-BAND_HALF, BAND_HALF + 1, dtype=jnp.int32) * dil
        bias = t5_bias[_t5_bucket(off)].T.astype(F32)
        variants = []
        for shift in (-BAND_HALF, 0, -2 * BAND_HALF):
            delta = rel + shift
            inside = np.abs(delta) <= BAND_HALF
            diag = jnp.where(inside[None], bias[:, np.clip(delta + BAND_HALF, 0, 2 * BAND_HALF)], NEG_INF)
            variants.append(_toeplitz(diag, QBLK_A, 2 * QBLK_A))
        band.append(jnp.stack(variants))
    return {"rope_b": rope_b, "rope_d": rope_d, "seg": seg, "band": band}


def _toeplitz(w, n_rows, n_cols):
    period = w.shape[-1]
    tiled = jnp.tile(w, (1,) * (w.ndim - 1) + (n_rows,))
    return tiled[..., :n_rows * (period - 1)].reshape(w.shape[:-1] + (n_rows, period - 1))[..., :n_cols]


def _na_table(rpb):
    cols = np.arange(GRID_W)
    cs = np.clip(cols - NA_COLS // 2, 0, GRID_W - NA_COLS)
    inside = (cols[None, :] >= cs[:, None]) & (cols[None, :] < cs[:, None] + NA_COLS)
    period = 2 * GRID_W
    u = np.arange(period)
    dc = np.where(u < GRID_W, u, u - period) + NA_COLS - 1
    diag = jnp.where(((dc >= 0) & (dc <= 2 * NA_COLS - 2))[None, None],
                     rpb.astype(F32)[:, :, np.clip(dc, 0, 2 * NA_COLS - 2)], NEG_INF)
    vals = jnp.where(inside[None, None], _toeplitz(diag, GRID_W, GRID_W), NEG_INF)
    tabs = [jnp.transpose(vals[:, v:v + NA_ROWS], (0, 2, 1, 3)).reshape(N_HEADS, GRID_W, NA_ROWS * GRID_W)
            for v in range(NA_ROWS)]
    return jnp.stack(tabs)


def _layer_weights(l, norm_mix, w_in, b_q_gain, b_k_gain, d_q_gain, d_w_uq, d_kv_gain, d_w_ukv,
                   out_gain, w_out, norm_ffn, w_gate, w_up, w_down):
    d_in = w_in.shape[-1]
    kr_lo = d_in - D_ROPE
    w_in_p = jnp.zeros((D_MODEL, D_IN_PAD), F32)
    w_in_p = w_in_p.at[:, :kr_lo].set(w_in[l][:, :kr_lo])
    w_in_p = w_in_p.at[:, kr_lo + D_NOPE:kr_lo + D_NOPE + D_ROPE].set(w_in[l][:, kr_lo:])
    dqk = D_NOPE + D_ROPE
    w_uq = jnp.zeros((D_Q_LORA, N_HEADS, LANES), F32).at[:, :, :dqk].set(
        d_w_uq[l].reshape(D_Q_LORA, N_HEADS, dqk)).reshape(D_Q_LORA, N_HEADS * LANES)
    ukv = d_w_ukv[l].reshape(D_KV_LORA, N_HEADS, D_NOPE + D_V)
    w_uk = jnp.zeros((D_KV_LORA, N_HEADS, LANES), F32).at[:, :, :D_NOPE].set(
        ukv[:, :, :D_NOPE]).reshape(D_KV_LORA, N_HEADS * LANES)
    w_uv = ukv[:, :, D_NOPE:].reshape(D_KV_LORA, N_HEADS * D_V)
    return {
        "norm_mix": norm_mix[l][None, :], "w_in": w_in_p.astype(BF16),
        "b_q_gain": jnp.tile(b_q_gain[l], 2)[None, :], "b_k_gain": jnp.tile(b_k_gain[l], 2)[None, :],
        "d_q_gain": d_q_gain[l][None, :], "w_uq": w_uq.astype(BF16),
        "d_kv_gain": d_kv_gain[l][None, :], "w_uk": w_uk.astype(BF16), "w_uv": w_uv.astype(BF16),
        "out_gain": out_gain[l][None, :], "w_out": w_out[l].astype(BF16), "norm_ffn": norm_ffn[l][None, :],
        "w_gate": w_gate[l].astype(BF16), "w_up": w_up[l].astype(BF16), "w_down": w_down[l].astype(BF16),
    }


def _trunk(x, layers, tabs, na_tabs, final_norm):
    nb = x.shape[0]
    x2d = x.reshape(nb * SEQ, D_MODEL)
    for l, lw in enumerate(layers):
        a_qkv, b_qt, b_k, b_vt, c_qkv, d_qt, d_k, d_vt = _proj(x2d, lw, tabs)
        outs_a = tuple(_attn_a(a_qkv, tabs["band"][p], nb, dil)
                       for p, (_, dil) in enumerate(DILATED_PATTERNS))
        o_b = _attn_b(b_qt, b_k, b_vt, nb)
        o_c = _attn_c(c_qkv, na_tabs[l], nb)
        o_d = _attn_d(d_qt, d_k, d_vt, nb)
        x2d = _out_ffn(x2d, outs_a + (o_b, o_c, o_d), lw, final_norm, final=(l == DEPTH - 1))
    return x2d.reshape(nb, SEQ, D_MODEL)


def kernel(x_prompt, x_sample, t5_bias, norm_mix, w_in, b_q_gain, b_k_gain, c_rpb, d_q_gain, d_w_uq,
           d_kv_gain, d_w_ukv, out_gain, w_out, norm_ffn, w_gate, w_up, w_down, final_norm):
    tabs = _tables(t5_bias)
    na_tabs = [_na_table(c_rpb[l]) for l in range(DEPTH)]
    layers = [_layer_weights(l, norm_mix, w_in, b_q_gain, b_k_gain, d_q_gain, d_w_uq, d_kv_gain, d_w_ukv,
                             out_gain, w_out, norm_ffn, w_gate, w_up, w_down) for l in range(DEPTH)]
    fin = final_norm[None, :]
    return (_trunk(x_prompt, layers, tabs, na_tabs, fin), _trunk(x_sample, layers, tabs, na_tabs, fin))
```

```python
import functools
import math

import jax
import jax.numpy as jnp
import numpy as np
from jax import lax
from jax.experimental import pallas as pl
from jax.experimental.pallas import tpu as pltpu

F32 = jnp.float32
BF16 = jnp.bfloat16

D_MODEL = 1024
SEQ = 4096
DEPTH = 2
HEAD_DIM = 64
N_HEADS = 4
GROUP_WIDTH = N_HEADS * HEAD_DIM
DILATED_PATTERNS = ((128, 1), (512, 4), (2048, 16))
BAND_HALF = 64
T5_BUCKETS = 32
T5_MAX_DIST = 1024
GRID_W = 64
GRID_ROWS = SEQ // GRID_W
NA_ROWS = 8
NA_COLS = 16
D_Q_LORA = 256
D_KV_LORA = 128
D_NOPE = 64
D_ROPE = 32
D_V = 64
ROPE_THETA = 10000.0
D_FF = 2816
RMS_EPS = 1e-6
NEG_INF = -1e30

LANES = 128
D_IN_PAD = 2560
COL_A, COL_B, COL_C, COL_D = 0, 768, 1280, 2048
A_WIDTH = 1024
ATTN_SCALE = HEAD_DIM ** -0.5
MLA_SCALE = (D_NOPE + D_ROPE) ** -0.5
LOG2E = math.log2(math.e)

TM_PROJ = 512
QBLK_A = 128
TQ_DENSE = 256
TK_DENSE = 512
ONES_ROWS = 16
ROWS_C = 4
WIN_ROWS_C = 12
TM_FFN = 512
FF_BOUNDS = (0, 768, 1536, 2304, D_FF)
VMEM_LIMIT = 56 * 1024 * 1024


def _cparams(n_axes):
    return pltpu.CompilerParams(dimension_semantics=("arbitrary",) * n_axes,
                                vmem_limit_bytes=VMEM_LIMIT)


def _rms(x, g):
    return x * lax.rsqrt(jnp.mean(x * x, axis=-1, keepdims=True) + RMS_EPS) * g


def _dot(a, b):
    return jnp.dot(a, b, preferred_element_type=F32)


def _dot_nt(a, b):
    return lax.dot_general(a, b, (((1,), (1,)), ((), ())), preferred_element_type=F32)


def _rope(y, tab):
    return y * tab[0] + pltpu.roll(y, LANES - 16, 1) * tab[1] + pltpu.roll(y, 16, 1) * tab[2]


def _proj_kernel(x_ref, gmix_ref, w_ref, bqg_ref, bkg_ref, seg_ref, tabb_ref, tabd_ref,
                 dqg_ref, wuq_ref, dkvg_ref, wuk_ref, wuv_ref,
                 a_ref, bqt_ref, bk_ref, bvt_ref, cqt_ref, ck_ref, cvt_ref, dqt_ref, dk_ref, dvt_ref):
    h = _rms(x_ref[...], gmix_ref[...]).astype(BF16)

    def proj(lo, hi):
        return _dot(h, w_ref[:, lo:hi])

    pa = proj(COL_A, COL_B)
    low_half = lax.broadcasted_iota(jnp.int32, (TM_PROJ, LANES), 1) < HEAD_DIM
    for hd in range(N_HEADS):
        pair = pa[:, (hd // 2) * LANES:(hd // 2 + 1) * LANES] * (ATTN_SCALE * LOG2E)
        own = low_half if hd % 2 == 0 else jnp.logical_not(low_half)
        a_ref[:, hd * LANES:(hd + 1) * LANES] = jnp.where(own, pair, 0.0).astype(BF16)
    a_ref[:, 2 * GROUP_WIDTH:] = pa[:, GROUP_WIDTH:].astype(BF16)

    pc = proj(COL_C, COL_D)
    zeros_half = jnp.zeros((HEAD_DIM, TM_PROJ), BF16)
    cqt = (pc[:, :GROUP_WIDTH] * (ATTN_SCALE * LOG2E)).T.astype(BF16)
    for hd in range(N_HEADS):
        own = hd * LANES + (hd % 2) * HEAD_DIM
        other = hd * LANES + (1 - hd % 2) * HEAD_DIM
        cqt_ref[own:own + HEAD_DIM, :] = cqt[hd * HEAD_DIM:(hd + 1) * HEAD_DIM]
        cqt_ref[other:other + HEAD_DIM, :] = zeros_half
    ck_ref[...] = pc[:, GROUP_WIDTH:2 * GROUP_WIDTH].astype(BF16)
    cvt_ref[...] = pc[:, 2 * GROUP_WIDTH:].T.astype(BF16)

    pb = proj(COL_B, COL_C)
    seg = seg_ref[...]
    tabb = tabb_ref[...]

    def head_norm_rope(chunk, gain):
        sq = chunk * chunk
        hi = sq.astype(BF16)
        lo = (sq - hi.astype(F32)).astype(BF16)
        ms = _dot(hi, seg) + _dot(lo, seg)
        return _rope(chunk * lax.rsqrt(ms + RMS_EPS) * gain, tabb)

    for c in range(2):
        q = head_norm_rope(pb[:, c * LANES:(c + 1) * LANES], bqg_ref[...])
        qt = (q * (ATTN_SCALE * LOG2E)).T.astype(BF16)
        for j in range(2):
            r0 = (2 * c + j) * LANES
            bqt_ref[r0:r0 + HEAD_DIM, :] = qt[j * HEAD_DIM:(j + 1) * HEAD_DIM]
            bqt_ref[r0 + HEAD_DIM:r0 + LANES, :] = zeros_half
    kb = head_norm_rope(pb[:, 2 * LANES:3 * LANES], bkg_ref[...])
    bk_ref[:, :LANES] = kb.astype(BF16)
    bk_ref[:, LANES:] = pltpu.roll(kb, HEAD_DIM, 1).astype(BF16)
    bvt_ref[...] = pb[:, 3 * LANES:].T.astype(BF16)

    pd = proj(COL_D, D_IN_PAD)
    tabd = tabd_ref[...]
    dq = _rms(pd[:, :D_Q_LORA], dqg_ref[...]).astype(BF16)
    qd = _dot(dq, wuq_ref[...])
    dkv = _rms(pd[:, D_Q_LORA:D_Q_LORA + D_KV_LORA], dkvg_ref[...]).astype(BF16)
    kn = _dot(dkv, wuk_ref[...])
    dvt_ref[...] = _dot(dkv, wuv_ref[...]).T.astype(BF16)
    kr = _rope(pd[:, D_Q_LORA + D_KV_LORA:], tabd)
    for hd in range(N_HEADS):
        sl = slice(hd * LANES, (hd + 1) * LANES)
        dqt_ref[sl, :] = (_rope(qd[:, sl], tabd) * (MLA_SCALE * LOG2E)).T.astype(BF16)
        dk_ref[:, sl] = (kn[:, sl] + kr).astype(BF16)


def _proj(x2d, lw, tabs):
    t = x2d.shape[0]
    blocks_per_seq = SEQ // TM_PROJ
    row = lambda i: (i, 0)
    const2 = lambda i: (0, 0)
    pos3 = lambda i: (0, i % blocks_per_seq, 0)

    def full(a):
        return pl.BlockSpec(a.shape, const2)

    col = lambda i: (0, i)
    outs = ((A_WIDTH, False), (512, True), (256, False), (128, True), (512, True), (256, False), (256, True),
            (512, True), (512, False), (256, True))
    return pl.pallas_call(
        _proj_kernel,
        grid=(t // TM_PROJ,),
        in_specs=[pl.BlockSpec((TM_PROJ, D_MODEL), row), full(lw["norm_mix"]), full(lw["w_in"]),
                  full(lw["b_q_gain"]), full(lw["b_k_gain"]), full(tabs["seg"]),
                  pl.BlockSpec((3, TM_PROJ, LANES), pos3), pl.BlockSpec((3, TM_PROJ, LANES), pos3),
                  full(lw["d_q_gain"]), full(lw["w_uq"]), full(lw["d_kv_gain"]), full(lw["w_uk"]),
                  full(lw["w_uv"])],
        out_specs=[pl.BlockSpec((w, TM_PROJ), col) if tr else pl.BlockSpec((TM_PROJ, w), row) for w, tr in outs],
        out_shape=[jax.ShapeDtypeStruct((w, t) if tr else (t, w), BF16) for w, tr in outs],
        compiler_params=_cparams(1),
        name="proj",
    )(x2d, lw["norm_mix"], lw["w_in"], lw["b_q_gain"], lw["b_k_gain"], tabs["seg"], tabs["rope_b"],
      tabs["rope_d"], lw["d_q_gain"], lw["w_uq"], lw["d_kv_gain"], lw["w_uk"], lw["w_uv"])


def _attn_a_kernel(q_ref, k_ref, v_ref, tab_ref, o_ref, st_ref, *, n):
    nblk = n // QBLK_A
    lane = lax.broadcasted_iota(jnp.int32, (QBLK_A, LANES), 1)
    ones = jnp.ones((2 * QBLK_A, LANES), BF16)

    def body(i, carry):
        rows = pl.ds(pl.multiple_of(i * QBLK_A, QBLK_A), QBLK_A)
        win = pl.ds(pl.multiple_of(jnp.clip(i * QBLK_A - BAND_HALF, 0, n - 2 * QBLK_A), BAND_HALF), 2 * QBLK_A)
        var = jnp.where(i == 0, 1, jnp.where(i == nblk - 1, 2, 0))

        def scores(hd):
            pair = slice((hd // 2) * LANES, (hd // 2 + 1) * LANES)
            return _dot_nt(q_ref[rows, hd * LANES:(hd + 1) * LANES], k_ref[win, pair]) + tab_ref[var, hd]

        st = jnp.zeros((QBLK_A, LANES), F32)
        s_next = scores(0)
        for hd in range(N_HEADS):
            s = s_next
            if hd + 1 < N_HEADS:
                s_next = scores(hd + 1)
            pair = slice((hd // 2) * LANES, (hd // 2 + 1) * LANES)
            m = jnp.max(s, axis=-1, keepdims=True)
            e = jnp.exp2(s - m).astype(BF16)
            pv = _dot(e, jnp.concatenate([v_ref[win, pair], ones], axis=1))
            l = pv[:, LANES:]
            o_h = pv[:, :LANES] / l
            st = jnp.where(lane == hd, m + jnp.log2(l), st)
            if hd % 2 == 0:
                o_even = o_h
            else:
                o_ref[rows, pair] = jnp.where(lane < HEAD_DIM, o_even, o_h)
        st_ref[rows, :] = st
        return carry

    lax.fori_loop(0, nblk, body, 0)


def _attn_a(a_qkv, tab, nb, dil):
    n = SEQ // dil
    view = a_qkv.reshape(nb, n, dil * A_WIDTH)
    o, st = pl.pallas_call(
        functools.partial(_attn_a_kernel, n=n),
        grid=(nb, dil),
        in_specs=[pl.BlockSpec((None, n, 2 * GROUP_WIDTH), lambda b, r: (b, 0, 2 * r)),
                  pl.BlockSpec((None, n, GROUP_WIDTH), lambda b, r: (b, 0, 4 * r + 2)),
                  pl.BlockSpec((None, n, GROUP_WIDTH), lambda b, r: (b, 0, 4 * r + 3)),
                  pl.BlockSpec(tab.shape, lambda b, r: (0, 0, 0, 0))],
        out_specs=[pl.BlockSpec((None, n, GROUP_WIDTH), lambda b, r: (b, 0, r)),
                   pl.BlockSpec((None, n, LANES), lambda b, r: (b, 0, r))],
        out_shape=[jax.ShapeDtypeStruct((nb, n, dil * GROUP_WIDTH), F32),
                   jax.ShapeDtypeStruct((nb, n, dil * LANES), F32)],
        compiler_params=_cparams(2),
        name=f"attn_a_d{dil}",
    )(view, view, view, tab)
    return o.reshape(nb * SEQ, GROUP_WIDTH), st.reshape(nb * SEQ, LANES)


def _flash(problems):
    n_chunks = SEQ // TK_DENSE
    ones = jnp.ones((ONES_ROWS, TK_DENSE), BF16)
    state = [None] * len(problems)
    s_cur = [_dot(k_at(0), q_t) for q_t, k_at, _ in problems]
    for c in range(n_chunks):
        ps, ms = [], []
        for i, s in enumerate(s_cur):
            col_max = jnp.max(s, axis=0, keepdims=True)
            m_new = col_max if c == 0 else jnp.maximum(state[i][0], col_max)
            ms.append(m_new)
            ps.append(jnp.exp2(s - m_new).astype(BF16))
        if c + 1 < n_chunks:
            s_next = [_dot(k_at(c + 1), q_t) for q_t, k_at, _ in problems]
        for i, (p, m_new, (_, _, v_t_at)) in enumerate(zip(ps, ms, problems)):
            v_ext = jnp.concatenate([v_t_at(c), ones], axis=0)
            if c == 0:
                acc = _dot(v_ext, p)
            else:
                m_prev, acc_prev = state[i]
                acc = jnp.exp2(m_prev - m_new) * acc_prev + _dot(v_ext, p)
            state[i] = (m_new, acc)
        s_cur = s_next
    outs = []
    for (_, _, v_t_at), (_, acc) in zip(problems, state):
        dv = acc.shape[0] - ONES_ROWS
        outs.append(acc[:dv] / acc[dv:dv + 1])
    return outs


def _chunk(c):
    return pl.ds(c * TK_DENSE, TK_DENSE)


def _attn_b_kernel(qt_ref, k_ref, vt_ref, o_ref):
    problems = []
    for g in range(2):
        q_t = jnp.concatenate([qt_ref[(2 * g) * LANES:(2 * g + 1) * LANES, :],
                               qt_ref[(2 * g + 1) * LANES:(2 * g + 2) * LANES, :]], axis=1)
        problems.append((q_t,
                         lambda c, g=g: k_ref[_chunk(c), g * LANES:(g + 1) * LANES],
                         lambda c, g=g: vt_ref[g * HEAD_DIM:(g + 1) * HEAD_DIM, _chunk(c)]))
    outs = _flash(problems)
    o_t = jnp.concatenate([o[:, j * TQ_DENSE:(j + 1) * TQ_DENSE] for o in outs for j in range(2)], axis=0)
    o_ref[...] = o_t.T


def _attn_b(b_qt, b_k, b_vt, nb):
    nq = SEQ // TQ_DENSE
    o = pl.pallas_call(
        _attn_b_kernel,
        grid=(nb, nq),
        in_specs=[pl.BlockSpec((N_HEADS * LANES, TQ_DENSE), lambda b, i: (0, b * nq + i)),
                  pl.BlockSpec((None, SEQ, 2 * LANES), lambda b, i: (b, 0, 0)),
                  pl.BlockSpec((2 * HEAD_DIM, SEQ), lambda b, i: (0, b))],
        out_specs=pl.BlockSpec((None, TQ_DENSE, GROUP_WIDTH), lambda b, i: (b, i, 0)),
        out_shape=jax.ShapeDtypeStruct((nb, SEQ, GROUP_WIDTH), F32),
        compiler_params=_cparams(2),
        name="attn_b",
    )(b_qt, b_k.reshape(nb, SEQ, 2 * LANES), b_vt)
    return o.reshape(nb * SEQ, GROUP_WIDTH)


def _attn_d_kernel(qt_ref, k_ref, vt_ref, o_ref):
    problems = []
    for j in range(2):
        problems.append((qt_ref[j * LANES:(j + 1) * LANES, :],
                         lambda c, j=j: k_ref[_chunk(c), j * LANES:(j + 1) * LANES],
                         lambda c, j=j: vt_ref[j * D_V:(j + 1) * D_V, _chunk(c)]))
    o_ref[...] = jnp.concatenate(_flash(problems), axis=0).T


def _attn_d(d_qt, d_k, d_vt, nb):
    nq = SEQ // TQ_DENSE
    o = pl.pallas_call(
        _attn_d_kernel,
        grid=(nb, 2, nq),
        in_specs=[pl.BlockSpec((2 * LANES, TQ_DENSE), lambda b, p, i: (p, b * nq + i)),
                  pl.BlockSpec((None, SEQ, 2 * LANES), lambda b, p, i: (b, 0, p)),
                  pl.BlockSpec((2 * D_V, SEQ), lambda b, p, i: (p, b))],
        out_specs=pl.BlockSpec((None, TQ_DENSE, LANES), lambda b, p, i: (b, i, p)),
        out_shape=jax.ShapeDtypeStruct((nb, SEQ, GROUP_WIDTH), F32),
        compiler_params=_cparams(3),
        name="attn_d",
    )(d_qt, d_k.reshape(nb, SEQ, N_HEADS * LANES), d_vt)
    return o.reshape(nb * SEQ, GROUP_WIDTH)


def _attn_c_kernel(qt_ref, k_ref, vt_ref, tab_ref, o_ref):
    i = pl.program_id(1)
    n_groups = GRID_ROWS // ROWS_C
    var = jnp.where(i == 0, 1, jnp.where(i == n_groups - 1, 2, 0))
    k0 = pl.multiple_of(jnp.clip(i * ROWS_C - NA_ROWS // 2, 0, GRID_ROWS - WIN_ROWS_C) * GRID_W,
                        ROWS_C * GRID_W)
    keys = pl.ds(k0, WIN_ROWS_C * GRID_W)
    ones = jnp.ones((ONES_ROWS, WIN_ROWS_C * GRID_W), BF16)

    def scores(hd):
        pair = slice((hd // 2) * LANES, (hd // 2 + 1) * LANES)
        return _dot(k_ref[keys, pair], qt_ref[hd * LANES:(hd + 1) * LANES, :]) + tab_ref[var, hd]

    outs = []
    s_next = scores(0)
    for hd in range(N_HEADS):
        s = s_next
        if hd + 1 < N_HEADS:
            s_next = scores(hd + 1)
        p = jnp.exp2(s - jnp.max(s, axis=0, keepdims=True)).astype(BF16)
        v_ext = jnp.concatenate([vt_ref[hd * HEAD_DIM:(hd + 1) * HEAD_DIM, keys], ones], axis=0)
        acc = _dot(v_ext, p)
        outs.append(acc[:HEAD_DIM] / acc[HEAD_DIM:HEAD_DIM + 1])
    o_ref[...] = jnp.concatenate(outs, axis=0).T


def _attn_c(c_qt, c_k, c_vt, tab, nb):
    tq = ROWS_C * GRID_W
    nq = SEQ // tq
    o = pl.pallas_call(
        _attn_c_kernel,
        grid=(nb, nq),
        in_specs=[pl.BlockSpec((N_HEADS * LANES, tq), lambda b, i: (0, b * nq + i)),
                  pl.BlockSpec((None, SEQ, GROUP_WIDTH), lambda b, i: (b, 0, 0)),
                  pl.BlockSpec((GROUP_WIDTH, SEQ), lambda b, i: (0, b)),
                  pl.BlockSpec(tab.shape, lambda b, i: (0, 0, 0, 0), pipeline_mode=pl.Buffered(1))],
        out_specs=pl.BlockSpec((None, tq, GROUP_WIDTH), lambda b, i: (b, i, 0)),
        out_shape=jax.ShapeDtypeStruct((nb, SEQ, GROUP_WIDTH), F32),
        compiler_params=_cparams(2),
        name="attn_c",
    )(c_qt, c_k.reshape(nb, SEQ, GROUP_WIDTH), c_vt, tab)
    return o.reshape(nb * SEQ, GROUP_WIDTH)


def _out_ffn_kernel(x_ref, oa1_ref, sa1_ref, oa2_ref, sa2_ref, oa3_ref, sa3_ref, ob_ref, oc_ref, od_ref,
                    expand_ref, og_ref, wout_ref, nf_ref, wg_ref, wu_ref, wd_ref, fin_ref, y_ref, *, final):
    lses = [sa1_ref[...], sa2_ref[...], sa3_ref[...]]
    mx = jnp.maximum(jnp.maximum(lses[0], lses[1]), lses[2])
    es = [jnp.exp2(s - mx) for s in lses]
    den = es[0] + es[1] + es[2]
    expand = expand_ref[...]

    def per_head(w):
        hi = w.astype(BF16)
        lo = (w - hi.astype(F32)).astype(BF16)
        return _dot(hi, expand) + _dot(lo, expand)

    o_a = (per_head(es[0] / den) * oa1_ref[...] + per_head(es[1] / den) * oa2_ref[...]
           + per_head(es[2] / den) * oa3_ref[...])

    x = x_ref[...]
    groups = (o_a, ob_ref[...], oc_ref[...], od_ref[...])
    normed = [_rms(o, og_ref[:, g * GROUP_WIDTH:(g + 1) * GROUP_WIDTH]).astype(BF16) for g, o in enumerate(groups)]
    x1 = x + _dot(jnp.concatenate(normed, axis=1), wout_ref[...])

    h = _rms(x1, nf_ref[...]).astype(BF16)

    def gate_up(c):
        cols = slice(FF_BOUNDS[c], FF_BOUNDS[c + 1])
        gate = _dot(h, wg_ref[:, cols])
        up = _dot(h, wu_ref[:, cols])
        return (gate * (1.0 / (1.0 + jnp.exp(-gate))) * up).astype(BF16)

    n_ff = len(FF_BOUNDS) - 1
    act = gate_up(0)
    ff = None
    for c in range(n_ff):
        act_next = gate_up(c + 1) if c + 1 < n_ff else None
        part = _dot(act, wd_ref[FF_BOUNDS[c]:FF_BOUNDS[c + 1], :])
        ff = part if ff is None else ff + part
        act = act_next
    x2 = x1 + ff
    if final:
        x2 = _rms(x2, fin_ref[...])
    y_ref[...] = x2


def _out_ffn(x2d, attn_outs, lw, expand, final_norm, final):
    t = x2d.shape[0]
    row = lambda i: (i, 0)

    def resident(a):
        return pl.BlockSpec(a.shape, lambda i: (0, 0), pipeline_mode=pl.Buffered(1))

    (oa1, sa1), (oa2, sa2), (oa3, sa3), ob, oc, od = attn_outs
    wide = pl.BlockSpec((TM_FFN, GROUP_WIDTH), row)
    stat = pl.BlockSpec((TM_FFN, LANES), row)
    weights = (expand, lw["out_gain"], lw["w_out"], lw["norm_ffn"], lw["w_gate"], lw["w_up"], lw["w_down"],
               final_norm)
    return pl.pallas_call(
        functools.partial(_out_ffn_kernel, final=final),
        grid=(t // TM_FFN,),
        in_specs=[pl.BlockSpec((TM_FFN, D_MODEL), row), wide, stat, wide, stat, wide, stat, wide, wide, wide]
                 + [resident(w) for w in weights],
        out_specs=pl.BlockSpec((TM_FFN, D_MODEL), row),
        out_shape=jax.ShapeDtypeStruct((t, D_MODEL), F32),
        compiler_params=_cparams(1),
        name="out_ffn",
    )(x2d, oa1, sa1, oa2, sa2, oa3, sa3, ob, oc, od, *weights)


def _rope_angles(pos, dim):
    inv = 1.0 / (ROPE_THETA ** (jnp.arange(0, dim, 2, dtype=F32) / dim))
    return pos.astype(F32)[:, None] * inv[None, :]


def _t5_bucket(rel):
    nb = T5_BUCKETS // 2
    max_exact = nb // 2
    n = jnp.abs(rel)
    n_f = jnp.maximum(n, max_exact).astype(F32)
    large = max_exact + (jnp.log(n_f / max_exact) / math.log(T5_MAX_DIST / max_exact)
                         * (nb - max_exact)).astype(jnp.int32)
    large = jnp.minimum(large, nb - 1)
    return jnp.where(rel > 0, nb, 0) + jnp.where(n < max_exact, n, large)


def _tables(t5_bias):
    t = jnp.arange(SEQ, dtype=jnp.int32)
    z16 = jnp.zeros((SEQ, 16), F32)

    def trio(parts):
        return jnp.stack([jnp.concatenate(p, axis=1) for p in parts])

    ang_r = _rope_angles(t // GRID_W, HEAD_DIM // 2)
    ang_c = _rope_angles(t % GRID_W, HEAD_DIM // 2)
    cr, sr, cc, sc = jnp.cos(ang_r), jnp.sin(ang_r), jnp.cos(ang_c), jnp.sin(ang_c)
    rope_b = trio(([cr, cr, cc, cc] * 2, [-sr, z16, -sc, z16] * 2, [z16, sr, z16, sc] * 2))

    ang_t = _rope_angles(t, D_ROPE)
    ct, st = jnp.cos(ang_t), jnp.sin(ang_t)
    ones64, z64, z32 = jnp.ones((SEQ, 64), F32), jnp.zeros((SEQ, 64), F32), jnp.zeros((SEQ, 32), F32)
    rope_d = trio(([ones64, ct, ct, z32], [z64, -st, z16, z32], [z64, z16, st, z32]))

    seg = jnp.kron(jnp.eye(2, dtype=F32), jnp.full((HEAD_DIM, HEAD_DIM), 1.0 / HEAD_DIM, F32)).astype(BF16)
    expand = jnp.zeros((LANES, GROUP_WIDTH), F32).at[:N_HEADS].set(
        jnp.kron(jnp.eye(N_HEADS, dtype=F32), jnp.ones((1, HEAD_DIM), F32))).astype(BF16)

    period = 4 * QBLK_A
    u = np.arange(period)
    rel = np.where(u < period - QBLK_A, u, u - period)
    band = []
    for _, dil in DILATED_PATTERNS:
        off = jnp.arange(-BAND_HALF, BAND_HALF + 1, dtype=jnp.int32) * dil
        bias = t5_bias[_t5_bucket(off)].T.astype(F32)
        variants = []
        for shift in (-BAND_HALF, 0, -2 * BAND_HALF):
            delta = rel + shift
            inside = np.abs(delta) <= BAND_HALF
            diag = jnp.where(inside[None], bias[:, np.clip(delta + BAND_HALF, 0, 2 * BAND_HALF)] * LOG2E,
                             NEG_INF)
            variants.append(_toeplitz(diag, QBLK_A, 2 * QBLK_A))
        band.append(jnp.stack(variants))
    return {"rope_b": rope_b, "rope_d": rope_d, "seg": seg, "expand": expand, "band": band}


def _toeplitz(w, n_rows, n_cols):
    period = w.shape[-1]
    tiled = jnp.tile(w, (1,) * (w.ndim - 1) + (n_rows,))
    return tiled[..., :n_rows * (period - 1)].reshape(w.shape[:-1] + (n_rows, period - 1))[..., :n_cols]


def _na_table(rpb):
    cols = np.arange(GRID_W)
    cs = np.clip(cols - NA_COLS // 2, 0, GRID_W - NA_COLS)
    inside = (cols[None, :] >= cs[:, None]) & (cols[None, :] < cs[:, None] + NA_COLS)
    period = 2 * GRID_W
    u = np.arange(period)
    dc = np.where(u < GRID_W, u, u - period) + NA_COLS - 1
    diag = jnp.where(((dc >= 0) & (dc <= 2 * NA_COLS - 2))[None, None],
                     rpb.astype(F32)[:, :, np.clip(dc, 0, 2 * NA_COLS - 2)], NEG_INF)
    vals = jnp.where(inside[None, None], _toeplitz(diag, GRID_W, GRID_W), NEG_INF)
    vals = jnp.swapaxes(vals, 2, 3) * LOG2E
    masked = jnp.full((N_HEADS, GRID_W, GRID_W), NEG_INF, F32)
    tabs = []
    for lo, step, base in ((0, 1, NA_ROWS // 2 - 1), (0, 0, NA_ROWS - 1), (NA_ROWS // 2, 0, -1)):
        rows = []
        for w in range(WIN_ROWS_C):
            per_a = [vals[:, w - a + base] if lo <= w - a * step < lo + NA_ROWS else masked
                     for a in range(ROWS_C)]
            rows.append(jnp.concatenate(per_a, axis=-1))
        tabs.append(jnp.concatenate(rows, axis=1))
    return jnp.stack(tabs)


def _layer_weights(l, norm_mix, w_in, b_q_gain, b_k_gain, d_q_gain, d_w_uq, d_kv_gain, d_w_ukv,
                   out_gain, w_out, norm_ffn, w_gate, w_up, w_down):
    d_in = w_in.shape[-1]
    kr_lo = d_in - D_ROPE
    w_in_p = jnp.zeros((D_MODEL, D_IN_PAD), F32)
    w_in_p = w_in_p.at[:, :kr_lo].set(w_in[l][:, :kr_lo])
    w_in_p = w_in_p.at[:, kr_lo + D_NOPE:kr_lo + D_NOPE + D_ROPE].set(w_in[l][:, kr_lo:])
    dqk = D_NOPE + D_ROPE
    w_uq = jnp.zeros((D_Q_LORA, N_HEADS, LANES), F32).at[:, :, :dqk].set(
        d_w_uq[l].reshape(D_Q_LORA, N_HEADS, dqk)).reshape(D_Q_LORA, N_HEADS * LANES)
    ukv = d_w_ukv[l].reshape(D_KV_LORA, N_HEADS, D_NOPE + D_V)
    w_uk = jnp.zeros((D_KV_LORA, N_HEADS, LANES), F32).at[:, :, :D_NOPE].set(
        ukv[:, :, :D_NOPE]).reshape(D_KV_LORA, N_HEADS * LANES)
    w_uv = ukv[:, :, D_NOPE:].reshape(D_KV_LORA, N_HEADS * D_V)
    return {
        "norm_mix": norm_mix[l][None, :], "w_in": w_in_p.astype(BF16),
        "b_q_gain": jnp.tile(b_q_gain[l], 2)[None, :], "b_k_gain": jnp.tile(b_k_gain[l], 2)[None, :],
        "d_q_gain": d_q_gain[l][None, :], "w_uq": w_uq.astype(BF16),
        "d_kv_gain": d_kv_gain[l][None, :], "w_uk": w_uk.astype(BF16), "w_uv": w_uv.astype(BF16),
        "out_gain": out_gain[l][None, :], "w_out": w_out[l].astype(BF16), "norm_ffn": norm_ffn[l][None, :],
        "w_gate": w_gate[l].astype(BF16), "w_up": w_up[l].astype(BF16), "w_down": w_down[l].astype(BF16),
    }


def _trunk(x, layers, tabs, na_tabs, final_norm):
    nb = x.shape[0]
    x2d = x.reshape(nb * SEQ, D_MODEL)
    for l, lw in enumerate(layers):
        a_qkv, b_qt, b_k, b_vt, c_qt, c_k, c_vt, d_qt, d_k, d_vt = _proj(x2d, lw, tabs)
        outs_a = tuple(_attn_a(a_qkv, tabs["band"][p], nb, dil)
                       for p, (_, dil) in enumerate(DILATED_PATTERNS))
        o_b = _attn_b(b_qt, b_k, b_vt, nb)
        o_c = _attn_c(c_qt, c_k, c_vt, na_tabs[l], nb)
        o_d = _attn_d(d_qt, d_k, d_vt, nb)
        x2d = _out_ffn(x2d, outs_a + (o_b, o_c, o_d), lw, tabs["expand"], final_norm, final=(l == DEPTH - 1))
    return x2d.reshape(nb, SEQ, D_MODEL)


def kernel(x_prompt, x_sample, t5_bias, norm_mix, w_in, b_q_gain, b_k_gain, c_rpb, d_q_gain, d_w_uq,
           d_kv_gain, d_w_ukv, out_gain, w_out, norm_ffn, w_gate, w_up, w_down, final_norm):
    tabs = _tables(t5_bias)
    na_tabs = [_na_table(c_rpb[l]) for l in range(DEPTH)]
    layers = [_layer_weights(l, norm_mix, w_in, b_q_gain, b_k_gain, d_q_gain, d_w_uq, d_kv_gain, d_w_ukv,
                             out_gain, w_out, norm_ffn, w_gate, w_up, w_down) for l in range(DEPTH)]
    fin = final_norm[None, :]
    return (_trunk(x_prompt, layers, tabs, na_tabs, fin), _trunk(x_sample, layers, tabs, na_tabs, fin))
```

```python
import functools
import math

import jax
import jax.numpy as jnp
import numpy as np
from jax import lax
from jax.experimental import pallas as pl
from jax.experimental.pallas import tpu as pltpu

F32 = jnp.float32
BF16 = jnp.bfloat16

D_MODEL = 1024
SEQ = 4096
DEPTH = 2
HEAD_DIM = 64
N_HEADS = 4
GROUP_WIDTH = N_HEADS * HEAD_DIM
DILATED_PATTERNS = ((128, 1), (512, 4), (2048, 16))
BAND_HALF = 64
T5_BUCKETS = 32
T5_MAX_DIST = 1024
GRID_W = 64
GRID_ROWS = SEQ // GRID_W
NA_ROWS = 8
NA_COLS = 16
D_Q_LORA = 256
D_KV_LORA = 128
D_NOPE = 64
D_ROPE = 32
D_V = 64
ROPE_THETA = 10000.0
D_FF = 2816
RMS_EPS = 1e-6
NEG_INF = -1e30

LANES = 128
D_IN_PAD = 2560
COL_A, COL_B, COL_C, COL_D = 0, 768, 1280, 2048
A_WIDTH = 1024
ATTN_SCALE = HEAD_DIM ** -0.5
MLA_SCALE = (D_NOPE + D_ROPE) ** -0.5
LOG2E = math.log2(math.e)

TM_PROJ = 512
PROJ_SPLIT = 2
QBLK_A = 128
UNROLL_A = 4
TQ_DENSE = 256
TQ_MLA = 256
TK_DENSE = 512
SUB_DENSE = 256
ONES_ROWS = 16
ROWS_C = 4
WIN_ROWS_C = 12
TM_FFN = 512
FF_BOUNDS = (0, 768, 1536, 2304, D_FF)
VMEM_LIMIT = 56 * 1024 * 1024


def _cparams(n_axes):
    return pltpu.CompilerParams(dimension_semantics=("arbitrary",) * n_axes,
                                vmem_limit_bytes=VMEM_LIMIT)


def _rms(x, g):
    return x * lax.rsqrt(jnp.mean(x * x, axis=-1, keepdims=True) + RMS_EPS) * g


def _dot(a, b):
    return jnp.dot(a, b, preferred_element_type=F32)


def _dot_nt(a, b):
    return lax.dot_general(a, b, (((1,), (1,)), ((), ())), preferred_element_type=F32)


def _rope(y, tab):
    return y * tab[0] + pltpu.roll(y, LANES - 16, 1) * tab[1] + pltpu.roll(y, 16, 1) * tab[2]


def _proj_kernel(x_ref, gmix_ref, w_ref, bqg_ref, bkg_ref, seg_ref, tabb_ref, tabd_ref,
                 dqg_ref, wuq_ref, dkvg_ref, wuk_ref, wuv_ref,
                 a_ref, bqt_ref, bk_ref, bvt_ref, cqt_ref, ck_ref, cvt_ref, dqt_ref, dk_ref, dvt_ref):
    half = TM_PROJ // PROJ_SPLIT
    low_half = lax.broadcasted_iota(jnp.int32, (half, LANES), 1) < HEAD_DIM
    zeros_half = jnp.zeros((HEAD_DIM, half), BF16)
    seg = seg_ref[...]

    def main(rows):
        h = _rms(x_ref[rows, :], gmix_ref[...]).astype(BF16)
        return [_dot(h, w_ref[:, lo:hi]) for lo, hi in ((COL_A, COL_B), (COL_B, COL_C), (COL_C, COL_D),
                                                        (COL_D, D_IN_PAD))]

    def finish(rows, pa, pb, pc, pd):
        for hd in range(N_HEADS):
            pair = pa[:, (hd // 2) * LANES:(hd // 2 + 1) * LANES] * (ATTN_SCALE * LOG2E)
            own = low_half if hd % 2 == 0 else jnp.logical_not(low_half)
            a_ref[rows, hd * LANES:(hd + 1) * LANES] = jnp.where(own, pair, 0.0).astype(BF16)
        a_ref[rows, 2 * GROUP_WIDTH:] = pa[:, GROUP_WIDTH:].astype(BF16)

        cqt = (pc[:, :GROUP_WIDTH] * (ATTN_SCALE * LOG2E)).T.astype(BF16)
        for hd in range(N_HEADS):
            own = hd * LANES + (hd % 2) * HEAD_DIM
            other = hd * LANES + (1 - hd % 2) * HEAD_DIM
            cqt_ref[own:own + HEAD_DIM, rows] = cqt[hd * HEAD_DIM:(hd + 1) * HEAD_DIM]
            cqt_ref[other:other + HEAD_DIM, rows] = zeros_half
        ck_ref[rows, :] = pc[:, GROUP_WIDTH:2 * GROUP_WIDTH].astype(BF16)
        cvt_ref[:, rows] = pc[:, 2 * GROUP_WIDTH:].T.astype(BF16)

        tabb = tabb_ref[:, rows, :]

        def head_norm_rope(chunk, gain):
            sq = chunk * chunk
            hi = sq.astype(BF16)
            lo = (sq - hi.astype(F32)).astype(BF16)
            ms = _dot(hi, seg) + _dot(lo, seg)
            return _rope(chunk * lax.rsqrt(ms + RMS_EPS) * gain, tabb)

        for c in range(2):
            q = head_norm_rope(pb[:, c * LANES:(c + 1) * LANES], bqg_ref[...])
            qt = (q * (ATTN_SCALE * LOG2E)).T.astype(BF16)
            for j in range(2):
                r0 = (2 * c + j) * LANES
                bqt_ref[r0:r0 + HEAD_DIM, rows] = qt[j * HEAD_DIM:(j + 1) * HEAD_DIM]
                bqt_ref[r0 + HEAD_DIM:r0 + LANES, rows] = zeros_half
        kb = head_norm_rope(pb[:, 2 * LANES:3 * LANES], bkg_ref[...])
        bk_ref[rows, :LANES] = kb.astype(BF16)
        bk_ref[rows, LANES:] = pltpu.roll(kb, HEAD_DIM, 1).astype(BF16)
        bvt_ref[:, rows] = pb[:, 3 * LANES:].T.astype(BF16)

        tabd = tabd_ref[:, rows, :]
        dq = _rms(pd[:, :D_Q_LORA], dqg_ref[...]).astype(BF16)
        qd = _dot(dq, wuq_ref[...])
        dkv = _rms(pd[:, D_Q_LORA:D_Q_LORA + D_KV_LORA], dkvg_ref[...]).astype(BF16)
        kn = _dot(dkv, wuk_ref[...])
        dvt_ref[:, rows] = _dot(dkv, wuv_ref[...]).T.astype(BF16)
        kr = _rope(pd[:, D_Q_LORA + D_KV_LORA:], tabd)
        for hd in range(N_HEADS):
            sl = slice(hd * LANES, (hd + 1) * LANES)
            dqt_ref[sl, rows] = (_rope(qd[:, sl], tabd) * (MLA_SCALE * LOG2E)).T.astype(BF16)
            dk_ref[rows, sl] = (kn[:, sl] + kr).astype(BF16)

    row_sets = [slice(j * half, (j + 1) * half) for j in range(PROJ_SPLIT)]
    mains = [main(rows) for rows in row_sets]
    for rows, (pa, pb, pc, pd) in zip(row_sets, mains):
        finish(rows, pa, pb, pc, pd)


def _proj(x2d, lw, tabs):
    t = x2d.shape[0]
    blocks_per_seq = SEQ // TM_PROJ
    row = lambda i: (i, 0)
    const2 = lambda i: (0, 0)
    pos3 = lambda i: (0, i % blocks_per_seq, 0)

    def full(a):
        return pl.BlockSpec(a.shape, const2)

    col = lambda i: (0, i)
    outs = ((A_WIDTH, False), (512, True), (256, False), (128, True), (512, True), (256, False), (256, True),
            (512, True), (512, False), (256, True))
    return pl.pallas_call(
        _proj_kernel,
        grid=(t // TM_PROJ,),
        in_specs=[pl.BlockSpec((TM_PROJ, D_MODEL), row), full(lw["norm_mix"]), full(lw["w_in"]),
                  full(lw["b_q_gain"]), full(lw["b_k_gain"]), full(tabs["seg"]),
                  pl.BlockSpec((3, TM_PROJ, LANES), pos3), pl.BlockSpec((3, TM_PROJ, LANES), pos3),
                  full(lw["d_q_gain"]), full(lw["w_uq"]), full(lw["d_kv_gain"]), full(lw["w_uk"]),
                  full(lw["w_uv"])],
        out_specs=[pl.BlockSpec((w, TM_PROJ), col) if tr else pl.BlockSpec((TM_PROJ, w), row) for w, tr in outs],
        out_shape=[jax.ShapeDtypeStruct((w, t) if tr else (t, w), BF16) for w, tr in outs],
        compiler_params=_cparams(1),
        name="proj",
    )(x2d, lw["norm_mix"], lw["w_in"], lw["b_q_gain"], lw["b_k_gain"], tabs["seg"], tabs["rope_b"],
      tabs["rope_d"], lw["d_q_gain"], lw["w_uq"], lw["d_kv_gain"], lw["w_uk"], lw["w_uv"])


def _attn_a_kernel(q_ref, k_ref, v_ref, tab_ref, o_ref, st_ref, *, n):
    nblk = n // QBLK_A
    lane = lax.broadcasted_iota(jnp.int32, (QBLK_A, LANES), 1)
    ones = jnp.ones((2 * QBLK_A, LANES), BF16)

    def body(i, carry):
        rows = pl.ds(pl.multiple_of(i * QBLK_A, QBLK_A), QBLK_A)
        win = pl.ds(pl.multiple_of(jnp.clip(i * QBLK_A - BAND_HALF, 0, n - 2 * QBLK_A), BAND_HALF), 2 * QBLK_A)
        var = jnp.where(i == 0, 1, jnp.where(i == nblk - 1, 2, 0))

        def scores(hd):
            pair = slice((hd // 2) * LANES, (hd // 2 + 1) * LANES)
            return _dot_nt(q_ref[rows, hd * LANES:(hd + 1) * LANES], k_ref[win, pair]) + tab_ref[var, hd]

        st = jnp.zeros((QBLK_A, LANES), F32)
        s_next = scores(0)
        for hd in range(N_HEADS):
            s = s_next
            if hd + 1 < N_HEADS:
                s_next = scores(hd + 1)
            pair = slice((hd // 2) * LANES, (hd // 2 + 1) * LANES)
            m = jnp.max(s, axis=-1, keepdims=True)
            e = jnp.exp2(s - m).astype(BF16)
            pv = _dot(e, jnp.concatenate([v_ref[win, pair], ones], axis=1))
            l = pv[:, LANES:]
            o_h = pv[:, :LANES] / l
            st = jnp.where(lane == hd, m + jnp.log2(l), st)
            if hd % 2 == 0:
                o_even = o_h
            else:
                o_ref[rows, pair] = jnp.where(lane < HEAD_DIM, o_even, o_h)
        st_ref[rows, :] = st
        return carry

    lax.fori_loop(0, nblk, body, 0, unroll=min(UNROLL_A, nblk))


def _attn_a(a_qkv, tab, nb, dil):
    n = SEQ // dil
    view = a_qkv.reshape(nb, n, dil * A_WIDTH)
    o, st = pl.pallas_call(
        functools.partial(_attn_a_kernel, n=n),
        grid=(nb, dil),
        in_specs=[pl.BlockSpec((None, n, 2 * GROUP_WIDTH), lambda b, r: (b, 0, 2 * r)),
                  pl.BlockSpec((None, n, GROUP_WIDTH), lambda b, r: (b, 0, 4 * r + 2)),
                  pl.BlockSpec((None, n, GROUP_WIDTH), lambda b, r: (b, 0, 4 * r + 3)),
                  pl.BlockSpec(tab.shape, lambda b, r: (0, 0, 0, 0))],
        out_specs=[pl.BlockSpec((None, n, GROUP_WIDTH), lambda b, r: (b, 0, r)),
                   pl.BlockSpec((None, n, LANES), lambda b, r: (b, 0, r))],
        out_shape=[jax.ShapeDtypeStruct((nb, n, dil * GROUP_WIDTH), F32),
                   jax.ShapeDtypeStruct((nb, n, dil * LANES), F32)],
        compiler_params=_cparams(2),
        name=f"attn_a_d{dil}",
    )(view, view, view, tab)
    return o.reshape(nb * SEQ, GROUP_WIDTH), st.reshape(nb * SEQ, LANES)


def _flash(problems):
    n_chunks = SEQ // TK_DENSE
    ones = jnp.ones((ONES_ROWS, TK_DENSE), BF16)
    n_sub = TK_DENSE // SUB_DENSE
    state = [None] * len(problems)

    def scores(c):
        return [[_dot(k_at(c)[j * SUB_DENSE:(j + 1) * SUB_DENSE], q_t) for j in range(n_sub)]
                for q_t, k_at, _ in problems]

    s_cur = scores(0)
    for c in range(n_chunks):
        if c + 1 < n_chunks:
            s_next = scores(c + 1)
        for i, (subs, (_, _, v_t_at)) in enumerate(zip(s_cur, problems)):
            col_max = functools.reduce(jnp.maximum, [jnp.max(s, axis=0, keepdims=True) for s in subs])
            m_new = col_max if c == 0 else jnp.maximum(state[i][0], col_max)
            v_ext = jnp.concatenate([v_t_at(c), ones], axis=0)
            acc = None if c == 0 else jnp.exp2(state[i][0] - m_new) * state[i][1]
            for j, s in enumerate(subs):
                part = _dot(v_ext[:, j * SUB_DENSE:(j + 1) * SUB_DENSE], jnp.exp2(s - m_new).astype(BF16))
                acc = part if acc is None else acc + part
            state[i] = (m_new, acc)
        s_cur = s_next
    outs = []
    for (_, _, v_t_at), (_, acc) in zip(problems, state):
        dv = acc.shape[0] - ONES_ROWS
        outs.append(acc[:dv] / acc[dv:dv + 1])
    return outs


def _chunk(c):
    return pl.ds(c * TK_DENSE, TK_DENSE)


def _attn_b_kernel(qt_ref, k_ref, vt_ref, o_ref):
    problems = []
    for g in range(2):
        q_t = jnp.concatenate([qt_ref[(2 * g) * LANES:(2 * g + 1) * LANES, :],
                               qt_ref[(2 * g + 1) * LANES:(2 * g + 2) * LANES, :]], axis=1)
        problems.append((q_t,
                         lambda c, g=g: k_ref[_chunk(c), g * LANES:(g + 1) * LANES],
                         lambda c, g=g: vt_ref[g * HEAD_DIM:(g + 1) * HEAD_DIM, _chunk(c)]))
    outs = _flash(problems)
    o_t = jnp.concatenate([o[:, j * TQ_DENSE:(j + 1) * TQ_DENSE] for o in outs for j in range(2)], axis=0)
    o_ref[...] = o_t.T


def _attn_b(b_qt, b_k, b_vt, nb):
    nq = SEQ // TQ_DENSE
    o = pl.pallas_call(
        _attn_b_kernel,
        grid=(nb, nq),
        in_specs=[pl.BlockSpec((N_HEADS * LANES, TQ_DENSE), lambda b, i: (0, b * nq + i)),
                  pl.BlockSpec((None, SEQ, 2 * LANES), lambda b, i: (b, 0, 0)),
                  pl.BlockSpec((2 * HEAD_DIM, SEQ), lambda b, i: (0, b))],
        out_specs=pl.BlockSpec((None, TQ_DENSE, GROUP_WIDTH), lambda b, i: (b, i, 0)),
        out_shape=jax.ShapeDtypeStruct((nb, SEQ, GROUP_WIDTH), F32),
        compiler_params=_cparams(2),
        name="attn_b",
    )(b_qt, b_k.reshape(nb, SEQ, 2 * LANES), b_vt)
    return o.reshape(nb * SEQ, GROUP_WIDTH)


def _attn_d_kernel(qt_ref, k_ref, vt_ref, o_ref):
    problems = []
    for j in range(2):
        problems.append((qt_ref[j * LANES:(j + 1) * LANES, :],
                         lambda c, j=j: k_ref[_chunk(c), j * LANES:(j + 1) * LANES],
                         lambda c, j=j: vt_ref[j * D_V:(j + 1) * D_V, _chunk(c)]))
    o_ref[...] = jnp.concatenate(_flash(problems), axis=0).T


def _attn_d(d_qt, d_k, d_vt, nb):
    nq = SEQ // TQ_MLA
    o = pl.pallas_call(
        _attn_d_kernel,
        grid=(nb, 2, nq),
        in_specs=[pl.BlockSpec((2 * LANES, TQ_MLA), lambda b, p, i: (p, b * nq + i)),
                  pl.BlockSpec((None, SEQ, 2 * LANES), lambda b, p, i: (b, 0, p)),
                  pl.BlockSpec((2 * D_V, SEQ), lambda b, p, i: (p, b))],
        out_specs=pl.BlockSpec((None, TQ_MLA, LANES), lambda b, p, i: (b, i, p)),
        out_shape=jax.ShapeDtypeStruct((nb, SEQ, GROUP_WIDTH), F32),
        compiler_params=_cparams(3),
        name="attn_d",
    )(d_qt, d_k.reshape(nb, SEQ, N_HEADS * LANES), d_vt)
    return o.reshape(nb * SEQ, GROUP_WIDTH)


def _attn_c_kernel(qt_ref, k_ref, vt_ref, tab_ref, o_ref):
    i = pl.program_id(1)
    n_groups = GRID_ROWS // ROWS_C
    var = jnp.where(i == 0, 1, jnp.where(i == n_groups - 1, 2, 0))
    k0 = pl.multiple_of(jnp.clip(i * ROWS_C - NA_ROWS // 2, 0, GRID_ROWS - WIN_ROWS_C) * GRID_W,
                        ROWS_C * GRID_W)
    keys = pl.ds(k0, WIN_ROWS_C * GRID_W)
    ones = jnp.ones((ONES_ROWS, WIN_ROWS_C * GRID_W), BF16)

    def scores(hd):
        pair = slice((hd // 2) * LANES, (hd // 2 + 1) * LANES)
        return _dot(k_ref[keys, pair], qt_ref[hd * LANES:(hd + 1) * LANES, :]) + tab_ref[var, hd]

    outs = []
    s_next = scores(0)
    for hd in range(N_HEADS):
        s = s_next
        if hd + 1 < N_HEADS:
            s_next = scores(hd + 1)
        p = jnp.exp2(s - jnp.max(s, axis=0, keepdims=True)).astype(BF16)
        v_ext = jnp.concatenate([vt_ref[hd * HEAD_DIM:(hd + 1) * HEAD_DIM, keys], ones], axis=0)
        acc = _dot(v_ext, p)
        outs.append(acc[:HEAD_DIM] / acc[HEAD_DIM:HEAD_DIM + 1])
    o_ref[...] = jnp.concatenate(outs, axis=0).T


def _attn_c(c_qt, c_k, c_vt, tab, nb):
    tq = ROWS_C * GRID_W
    nq = SEQ // tq
    o = pl.pallas_call(
        _attn_c_kernel,
        grid=(nb, nq),
        in_specs=[pl.BlockSpec((N_HEADS * LANES, tq), lambda b, i: (0, b * nq + i)),
                  pl.BlockSpec((None, SEQ, GROUP_WIDTH), lambda b, i: (b, 0, 0)),
                  pl.BlockSpec((GROUP_WIDTH, SEQ), lambda b, i: (0, b)),
                  pl.BlockSpec(tab.shape, lambda b, i: (0, 0, 0, 0), pipeline_mode=pl.Buffered(1))],
        out_specs=pl.BlockSpec((None, tq, GROUP_WIDTH), lambda b, i: (b, i, 0)),
        out_shape=jax.ShapeDtypeStruct((nb, SEQ, GROUP_WIDTH), F32),
        compiler_params=_cparams(2),
        name="attn_c",
    )(c_qt, c_k.reshape(nb, SEQ, GROUP_WIDTH), c_vt, tab)
    return o.reshape(nb * SEQ, GROUP_WIDTH)


def _out_ffn_kernel(x_ref, oa1_ref, sa1_ref, oa2_ref, sa2_ref, oa3_ref, sa3_ref, ob_ref, oc_ref, od_ref,
                    expand_ref, og_ref, wout_ref, nf_ref, wg_ref, wu_ref, wd_ref, fin_ref, y_ref, *, final):
    lses = [sa1_ref[...], sa2_ref[...], sa3_ref[...]]
    mx = jnp.maximum(jnp.maximum(lses[0], lses[1]), lses[2])
    es = [jnp.exp2(s - mx) for s in lses]
    den = es[0] + es[1] + es[2]
    expand = expand_ref[...]

    def per_head(w):
        hi = w.astype(BF16)
        lo = (w - hi.astype(F32)).astype(BF16)
        return _dot(hi, expand) + _dot(lo, expand)

    o_a = (per_head(es[0] / den) * oa1_ref[...] + per_head(es[1] / den) * oa2_ref[...]
           + per_head(es[2] / den) * oa3_ref[...])

    x = x_ref[...]
    groups = (o_a, ob_ref[...], oc_ref[...], od_ref[...])
    normed = [_rms(o, og_ref[:, g * GROUP_WIDTH:(g + 1) * GROUP_WIDTH]).astype(BF16) for g, o in enumerate(groups)]
    x1 = x + _dot(jnp.concatenate(normed, axis=1), wout_ref[...])

    h = _rms(x1, nf_ref[...]).astype(BF16)

    def gate_up(c):
        cols = slice(FF_BOUNDS[c], FF_BOUNDS[c + 1])
        gate = _dot(h, wg_ref[:, cols])
        up = _dot(h, wu_ref[:, cols])
        return (gate * (1.0 / (1.0 + jnp.exp(-gate))) * up).astype(BF16)

    n_ff = len(FF_BOUNDS) - 1
    act = gate_up(0)
    ff = None
    for c in range(n_ff):
        act_next = gate_up(c + 1) if c + 1 < n_ff else None
        part = _dot(act, wd_ref[FF_BOUNDS[c]:FF_BOUNDS[c + 1], :])
        ff = part if ff is None else ff + part
        act = act_next
    x2 = x1 + ff
    if final:
        x2 = _rms(x2, fin_ref[...])
    y_ref[...] = x2


def _out_ffn(x2d, attn_outs, lw, expand, final_norm, final):
    t = x2d.shape[0]
    row = lambda i: (i, 0)

    def resident(a):
        return pl.BlockSpec(a.shape, lambda i: (0, 0), pipeline_mode=pl.Buffered(1))

    (oa1, sa1), (oa2, sa2), (oa3, sa3), ob, oc, od = attn_outs
    wide = pl.BlockSpec((TM_FFN, GROUP_WIDTH), row)
    stat = pl.BlockSpec((TM_FFN, LANES), row)
    weights = (expand, lw["out_gain"], lw["w_out"], lw["norm_ffn"], lw["w_gate"], lw["w_up"], lw["w_down"],
               final_norm)
    return pl.pallas_call(
        functools.partial(_out_ffn_kernel, final=final),
        grid=(t // TM_FFN,),
        in_specs=[pl.BlockSpec((TM_FFN, D_MODEL), row), wide, stat, wide, stat, wide, stat, wide, wide, wide]
                 + [resident(w) for w in weights],
        out_specs=pl.BlockSpec((TM_FFN, D_MODEL), row),
        out_shape=jax.ShapeDtypeStruct((t, D_MODEL), F32),
        compiler_params=_cparams(1),
        name="out_ffn",
    )(x2d, oa1, sa1, oa2, sa2, oa3, sa3, ob, oc, od, *weights)


def _rope_angles(pos, dim):
    inv = 1.0 / (ROPE_THETA ** (jnp.arange(0, dim, 2, dtype=F32) / dim))
    return pos.astype(F32)[:, None] * inv[None, :]


def _t5_bucket(rel):
    nb = T5_BUCKETS // 2
    max_exact = nb // 2
    n = jnp.abs(rel)
    n_f = jnp.maximum(n, max_exact).astype(F32)
    large = max_exact + (jnp.log(n_f / max_exact) / math.log(T5_MAX_DIST / max_exact)
                         * (nb - max_exact)).astype(jnp.int32)
    large = jnp.minimum(large, nb - 1)
    return jnp.where(rel > 0, nb, 0) + jnp.where(n < max_exact, n, large)


def _tables(t5_bias):
    t = jnp.arange(SEQ, dtype=jnp.int32)
    z16 = jnp.zeros((SEQ, 16), F32)

    def trio(parts):
        return jnp.stack([jnp.concatenate(p, axis=1) for p in parts])

    ang_r = _rope_angles(t // GRID_W, HEAD_DIM // 2)
    ang_c = _rope_angles(t % GRID_W, HEAD_DIM // 2)
    cr, sr, cc, sc = jnp.cos(ang_r), jnp.sin(ang_r), jnp.cos(ang_c), jnp.sin(ang_c)
    rope_b = trio(([cr, cr, cc, cc] * 2, [-sr, z16, -sc, z16] * 2, [z16, sr, z16, sc] * 2))

    ang_t = _rope_angles(t, D_ROPE)
    ct, st = jnp.cos(ang_t), jnp.sin(ang_t)
    ones64, z64, z32 = jnp.ones((SEQ, 64), F32), jnp.zeros((SEQ, 64), F32), jnp.zeros((SEQ, 32), F32)
    rope_d = trio(([ones64, ct, ct, z32], [z64, -st, z16, z32], [z64, z16, st, z32]))

    seg = jnp.kron(jnp.eye(2, dtype=F32), jnp.full((HEAD_DIM, HEAD_DIM), 1.0 / HEAD_DIM, F32)).astype(BF16)
    expand = jnp.zeros((LANES, GROUP_WIDTH), F32).at[:N_HEADS].set(
        jnp.kron(jnp.eye(N_HEADS, dtype=F32), jnp.ones((1, HEAD_DIM), F32))).astype(BF16)

    period = 4 * QBLK_A
    u = np.arange(period)
    rel = np.where(u < period - QBLK_A, u, u - period)
    band = []
    for _, dil in DILATED_PATTERNS:
        off = jnp.arange(-BAND_HALF, BAND_HALF + 1, dtype=jnp.int32) * dil
        bias = t5_bias[_t5_bucket(off)].T.astype(F32)
        variants = []
        for shift in (-BAND_HALF, 0, -2 * BAND_HALF):
            delta = rel + shift
            inside = np.abs(delta) <= BAND_HALF
            diag = jnp.where(inside[None], bias[:, np.clip(delta + BAND_HALF, 0, 2 * BAND_HALF)] * LOG2E,
                             NEG_INF)
            variants.append(_toeplitz(diag, QBLK_A, 2 * QBLK_A))
        band.append(jnp.stack(variants))
    return {"rope_b": rope_b, "rope_d": rope_d, "seg": seg, "expand": expand, "band": band}


def _toeplitz(w, n_rows, n_cols):
    period = w.shape[-1]
    tiled = jnp.tile(w, (1,) * (w.ndim - 1) + (n_rows,))
    return tiled[..., :n_rows * (period - 1)].reshape(w.shape[:-1] + (n_rows, period - 1))[..., :n_cols]


def _na_table(rpb):
    cols = np.arange(GRID_W)
    cs = np.clip(cols - NA_COLS // 2, 0, GRID_W - NA_COLS)
    inside = (cols[None, :] >= cs[:, None]) & (cols[None, :] < cs[:, None] + NA_COLS)
    period = 2 * GRID_W
    u = np.arange(period)
    dc = np.where(u < GRID_W, u, u - period) + NA_COLS - 1
    diag = jnp.where(((dc >= 0) & (dc <= 2 * NA_COLS - 2))[None, None],
                     rpb.astype(F32)[:, :, np.clip(dc, 0, 2 * NA_COLS - 2)], NEG_INF)
    vals = jnp.where(inside[None, None], _toeplitz(diag, GRID_W, GRID_W), NEG_INF)
    vals = jnp.swapaxes(vals, 2, 3) * LOG2E
    masked = jnp.full((N_HEADS, GRID_W, GRID_W), NEG_INF, F32)
    tabs = []
    for lo, step, base in ((0, 1, NA_ROWS // 2 - 1), (0, 0, NA_ROWS - 1), (NA_ROWS // 2, 0, -1)):
        rows = []
        for w in range(WIN_ROWS_C):
            per_a = [vals[:, w - a + base] if lo <= w - a * step < lo + NA_ROWS else masked
                     for a in range(ROWS_C)]
            rows.append(jnp.concatenate(per_a, axis=-1))
        tabs.append(jnp.concatenate(rows, axis=1))
    return jnp.stack(tabs)


def _layer_weights(l, norm_mix, w_in, b_q_gain, b_k_gain, d_q_gain, d_w_uq, d_kv_gain, d_w_ukv,
                   out_gain, w_out, norm_ffn, w_gate, w_up, w_down):
    d_in = w_in.shape[-1]
    kr_lo = d_in - D_ROPE
    w_in_p = jnp.zeros((D_MODEL, D_IN_PAD), F32)
    w_in_p = w_in_p.at[:, :kr_lo].set(w_in[l][:, :kr_lo])
    w_in_p = w_in_p.at[:, kr_lo + D_NOPE:kr_lo + D_NOPE + D_ROPE].set(w_in[l][:, kr_lo:])
    dqk = D_NOPE + D_ROPE
    w_uq = jnp.zeros((D_Q_LORA, N_HEADS, LANES), F32).at[:, :, :dqk].set(
        d_w_uq[l].reshape(D_Q_LORA, N_HEADS, dqk)).reshape(D_Q_LORA, N_HEADS * LANES)
    ukv = d_w_ukv[l].reshape(D_KV_LORA, N_HEADS, D_NOPE + D_V)
    w_uk = jnp.zeros((D_KV_LORA, N_HEADS, LANES), F32).at[:, :, :D_NOPE].set(
        ukv[:, :, :D_NOPE]).reshape(D_KV_LORA, N_HEADS * LANES)
    w_uv = ukv[:, :, D_NOPE:].reshape(D_KV_LORA, N_HEADS * D_V)
    return {
        "norm_mix": norm_mix[l][None, :], "w_in": w_in_p.astype(BF16),
        "b_q_gain": jnp.tile(b_q_gain[l], 2)[None, :], "b_k_gain": jnp.tile(b_k_gain[l], 2)[None, :],
        "d_q_gain": d_q_gain[l][None, :], "w_uq": w_uq.astype(BF16),
        "d_kv_gain": d_kv_gain[l][None, :], "w_uk": w_uk.astype(BF16), "w_uv": w_uv.astype(BF16),
        "out_gain": out_gain[l][None, :], "w_out": w_out[l].astype(BF16), "norm_ffn": norm_ffn[l][None, :],
        "w_gate": w_gate[l].astype(BF16), "w_up": w_up[l].astype(BF16), "w_down": w_down[l].astype(BF16),
    }


def _trunk(x, layers, tabs, na_tabs, final_norm):
    nb = x.shape[0]
    x2d = x.reshape(nb * SEQ, D_MODEL)
    for l, lw in enumerate(layers):
        a_qkv, b_qt, b_k, b_vt, c_qt, c_k, c_vt, d_qt, d_k, d_vt = _proj(x2d, lw, tabs)
        outs_a = tuple(_attn_a(a_qkv, tabs["band"][p], nb, dil)
                       for p, (_, dil) in enumerate(DILATED_PATTERNS))
        o_b = _attn_b(b_qt, b_k, b_vt, nb)
        o_c = _attn_c(c_qt, c_k, c_vt, na_tabs[l], nb)
        o_d = _attn_d(d_qt, d_k, d_vt, nb)
        x2d = _out_ffn(x2d, outs_a + (o_b, o_c, o_d), lw, tabs["expand"], final_norm, final=(l == DEPTH - 1))
    return x2d.reshape(nb, SEQ, D_MODEL)


def kernel(x_prompt, x_sample, t5_bias, norm_mix, w_in, b_q_gain, b_k_gain, c_rpb, d_q_gain, d_w_uq,
           d_kv_gain, d_w_ukv, out_gain, w_out, norm_ffn, w_gate, w_up, w_down, final_norm):
    tabs = _tables(t5_bias)
    na_tabs = [_na_table(c_rpb[l]) for l in range(DEPTH)]
    layers = [_layer_weights(l, norm_mix, w_in, b_q_gain, b_k_gain, d_q_gain, d_w_uq, d_kv_gain, d_w_ukv,
                             out_gain, w_out, norm_ffn, w_gate, w_up, w_down) for l in range(DEPTH)]
    fin = final_norm[None, :]
    return (_trunk(x_prompt, layers, tabs, na_tabs, fin), _trunk(x_sample, layers, tabs, na_tabs, fin))
```

```python
import functools
import math

import jax
import jax.numpy as jnp
import numpy as np
from jax import lax
from jax.experimental import pallas as pl
from jax.experimental.pallas import tpu as pltpu

F32 = jnp.float32
BF16 = jnp.bfloat16

D_MODEL = 1024
SEQ = 4096
DEPTH = 2
HEAD_DIM = 64
N_HEADS = 4
GROUP_WIDTH = N_HEADS * HEAD_DIM
DILATED_PATTERNS = ((128, 1), (512, 4), (2048, 16))
BAND_HALF = 64
DIL_MID, DIL_FAR = DILATED_PATTERNS[1][1], DILATED_PATTERNS[2][1]
T5_BUCKETS = 32
T5_MAX_DIST = 1024
GRID_W = 64
GRID_ROWS = SEQ // GRID_W
NA_ROWS = 8
NA_COLS = 16
D_Q_LORA = 256
D_KV_LORA = 128
D_NOPE = 64
D_ROPE = 32
D_V = 64
ROPE_THETA = 10000.0
D_FF = 2816
RMS_EPS = 1e-6
NEG_INF = -1e30

LANES = 128
D_IN_PAD = 2560
COL_A, COL_B, COL_C, COL_D = 0, 768, 1280, 2048
A_WIDTH = 1024
ATTN_SCALE = HEAD_DIM ** -0.5
MLA_SCALE = (D_NOPE + D_ROPE) ** -0.5
LOG2E = math.log2(math.e)

TM_PROJ = 512
PROJ_SPLIT = 2
QBLK_A = 128
UNROLL_A = 4
TQ_DENSE = 256
TQ_MLA = 256
TK_DENSE = 512
SUB_DENSE = 256
ONES_ROWS = 16
ROWS_C = 4
WIN_ROWS_C = 12
TM_FFN = 512
FF_BOUNDS = (0, 768, 1536, 2304, D_FF)
VMEM_LIMIT = 56 * 1024 * 1024


def _cparams(n_axes):
    return pltpu.CompilerParams(dimension_semantics=("arbitrary",) * n_axes,
                                vmem_limit_bytes=VMEM_LIMIT)


def _rms(x, g):
    return x * lax.rsqrt(jnp.mean(x * x, axis=-1, keepdims=True) + RMS_EPS) * g


def _dot(a, b):
    return jnp.dot(a, b, preferred_element_type=F32)


def _dot_nt(a, b):
    return lax.dot_general(a, b, (((1,), (1,)), ((), ())), preferred_element_type=F32)


def _rope(y, tab):
    return y * tab[0] + pltpu.roll(y, LANES - 16, 1) * tab[1] + pltpu.roll(y, 16, 1) * tab[2]


def _proj_kernel(x_ref, gmix_ref, w_ref, bqg_ref, bkg_ref, seg_ref, tabb_ref, tabd_ref,
                 dqg_ref, wuq_ref, dkvg_ref, wuk_ref, wuv_ref,
                 a_ref, a4_ref, a16_ref, bqt_ref, bk_ref, bvt_ref, cqt_ref, ck_ref, cvt_ref, dqt_ref, dk_ref,
                 dvt_ref, a_scr, a4_scr):
    half = TM_PROJ // PROJ_SPLIT
    low_half = lax.broadcasted_iota(jnp.int32, (half, LANES), 1) < HEAD_DIM
    zeros_half = jnp.zeros((HEAD_DIM, half), BF16)
    seg = seg_ref[...]

    def main(rows):
        h = _rms(x_ref[rows, :], gmix_ref[...]).astype(BF16)
        return [_dot(h, w_ref[:, lo:hi]) for lo, hi in ((COL_A, COL_B), (COL_B, COL_C), (COL_C, COL_D),
                                                        (COL_D, D_IN_PAD))]

    def finish(rows, pa, pb, pc, pd):
        slabs = []
        for hd in range(N_HEADS):
            pair = pa[:, (hd // 2) * LANES:(hd // 2 + 1) * LANES] * (ATTN_SCALE * LOG2E)
            own = low_half if hd % 2 == 0 else jnp.logical_not(low_half)
            slabs.append(jnp.where(own, pair, 0.0))
        slabs += [pa[:, GROUP_WIDTH + j * LANES:GROUP_WIDTH + (j + 1) * LANES] for j in range(4)]
        for s, val in enumerate(slabs):
            a_scr[s, rows, :] = val
            a_ref[rows, s * LANES:(s + 1) * LANES] = val.astype(BF16)

        cqt = (pc[:, :GROUP_WIDTH] * (ATTN_SCALE * LOG2E)).T.astype(BF16)
        for hd in range(N_HEADS):
            own = hd * LANES + (hd % 2) * HEAD_DIM
            other = hd * LANES + (1 - hd % 2) * HEAD_DIM
            cqt_ref[own:own + HEAD_DIM, rows] = cqt[hd * HEAD_DIM:(hd + 1) * HEAD_DIM]
            cqt_ref[other:other + HEAD_DIM, rows] = zeros_half
        ck_ref[rows, :] = pc[:, GROUP_WIDTH:2 * GROUP_WIDTH].astype(BF16)
        cvt_ref[:, rows] = pc[:, 2 * GROUP_WIDTH:].T.astype(BF16)

        tabb = tabb_ref[:, rows, :]

        def head_norm_rope(chunk, gain):
            sq = chunk * chunk
            hi = sq.astype(BF16)
            lo = (sq - hi.astype(F32)).astype(BF16)
            ms = _dot(hi, seg) + _dot(lo, seg)
            return _rope(chunk * lax.rsqrt(ms + RMS_EPS) * gain, tabb)

        for c in range(2):
            q = head_norm_rope(pb[:, c * LANES:(c + 1) * LANES], bqg_ref[...])
            qt = (q * (ATTN_SCALE * LOG2E)).T.astype(BF16)
            for j in range(2):
                r0 = (2 * c + j) * LANES
                bqt_ref[r0:r0 + HEAD_DIM, rows] = qt[j * HEAD_DIM:(j + 1) * HEAD_DIM]
                bqt_ref[r0 + HEAD_DIM:r0 + LANES, rows] = zeros_half
        kb = head_norm_rope(pb[:, 2 * LANES:3 * LANES], bkg_ref[...])
        bk_ref[rows, :LANES] = kb.astype(BF16)
        bk_ref[rows, LANES:] = pltpu.roll(kb, HEAD_DIM, 1).astype(BF16)
        bvt_ref[:, rows] = pb[:, 3 * LANES:].T.astype(BF16)

        tabd = tabd_ref[:, rows, :]
        dq = _rms(pd[:, :D_Q_LORA], dqg_ref[...]).astype(BF16)
        qd = _dot(dq, wuq_ref[...])
        dkv = _rms(pd[:, D_Q_LORA:D_Q_LORA + D_KV_LORA], dkvg_ref[...]).astype(BF16)
        kn = _dot(dkv, wuk_ref[...])
        dvt_ref[:, rows] = _dot(dkv, wuv_ref[...]).T.astype(BF16)
        kr = _rope(pd[:, D_Q_LORA + D_KV_LORA:], tabd)
        for hd in range(N_HEADS):
            sl = slice(hd * LANES, (hd + 1) * LANES)
            dqt_ref[sl, rows] = (_rope(qd[:, sl], tabd) * (MLA_SCALE * LOG2E)).T.astype(BF16)
            dk_ref[rows, sl] = (kn[:, sl] + kr).astype(BF16)

    row_sets = [slice(j * half, (j + 1) * half) for j in range(PROJ_SPLIT)]
    mains = [main(rows) for rows in row_sets]
    for rows, (pa, pb, pc, pd) in zip(row_sets, mains):
        finish(rows, pa, pb, pc, pd)

    n_slabs = A_WIDTH // LANES
    for r in range(DIL_MID):
        for s in range(n_slabs):
            val = a_scr[s, pl.ds(r, TM_PROJ // DIL_MID, stride=DIL_MID), :]
            a4_scr[r * n_slabs + s] = val
            a4_ref[:, (r * n_slabs + s) * LANES:(r * n_slabs + s + 1) * LANES] = val.astype(BF16)
    step = DIL_FAR // DIL_MID
    for j in range(step):
        for s in range(DIL_MID * n_slabs):
            c0 = (j * DIL_MID * n_slabs + s) * LANES
            a16_ref[:, c0:c0 + LANES] = a4_scr[s, pl.ds(j, TM_PROJ // DIL_FAR, stride=step), :].astype(BF16)


def _proj(x2d, lw, tabs):
    t = x2d.shape[0]
    blocks_per_seq = SEQ // TM_PROJ
    row = lambda i: (i, 0)
    const2 = lambda i: (0, 0)
    pos3 = lambda i: (0, i % blocks_per_seq, 0)

    def full(a):
        return pl.BlockSpec(a.shape, const2)

    col = lambda i: (0, i)
    outs = ((512, True), (256, False), (128, True), (512, True), (256, False), (256, True),
            (512, True), (512, False), (256, True))
    a_specs = [pl.BlockSpec((TM_PROJ // d, d * A_WIDTH), row) for _, d in DILATED_PATTERNS]
    a_shapes = [jax.ShapeDtypeStruct((t // d, d * A_WIDTH), BF16) for _, d in DILATED_PATTERNS]
    return pl.pallas_call(
        _proj_kernel,
        grid=(t // TM_PROJ,),
        scratch_shapes=[pltpu.VMEM((A_WIDTH // LANES, TM_PROJ, LANES), F32),
                        pltpu.VMEM((DIL_MID * A_WIDTH // LANES, TM_PROJ // DIL_MID, LANES), F32)],
        in_specs=[pl.BlockSpec((TM_PROJ, D_MODEL), row), full(lw["norm_mix"]), full(lw["w_in"]),
                  full(lw["b_q_gain"]), full(lw["b_k_gain"]), full(tabs["seg"]),
                  pl.BlockSpec((3, TM_PROJ, LANES), pos3), pl.BlockSpec((3, TM_PROJ, LANES), pos3),
                  full(lw["d_q_gain"]), full(lw["w_uq"]), full(lw["d_kv_gain"]), full(lw["w_uk"]),
                  full(lw["w_uv"])],
        out_specs=a_specs + [pl.BlockSpec((w, TM_PROJ), col) if tr else pl.BlockSpec((TM_PROJ, w), row)
                             for w, tr in outs],
        out_shape=a_shapes + [jax.ShapeDtypeStruct((w, t) if tr else (t, w), BF16) for w, tr in outs],
        compiler_params=_cparams(1),
        name="proj",
    )(x2d, lw["norm_mix"], lw["w_in"], lw["b_q_gain"], lw["b_k_gain"], tabs["seg"], tabs["rope_b"],
      tabs["rope_d"], lw["d_q_gain"], lw["w_uq"], lw["d_kv_gain"], lw["w_uk"], lw["w_uv"])


def _attn_a_kernel(q_ref, k_ref, v_ref, tab_ref, o_ref, st_ref, *, n):
    nblk = n // QBLK_A
    lane = lax.broadcasted_iota(jnp.int32, (QBLK_A, LANES), 1)
    ones = jnp.ones((2 * QBLK_A, LANES), BF16)

    def body(i, carry):
        rows = pl.ds(pl.multiple_of(i * QBLK_A, QBLK_A), QBLK_A)
        win = pl.ds(pl.multiple_of(jnp.clip(i * QBLK_A - BAND_HALF, 0, n - 2 * QBLK_A), BAND_HALF), 2 * QBLK_A)
        var = jnp.where(i == 0, 1, jnp.where(i == nblk - 1, 2, 0))

        def scores(hd):
            pair = slice((hd // 2) * LANES, (hd // 2 + 1) * LANES)
            return _dot_nt(q_ref[rows, hd * LANES:(hd + 1) * LANES], k_ref[win, pair]) + tab_ref[var, hd]

        st = jnp.zeros((QBLK_A, LANES), F32)
        s_next = scores(0)
        for hd in range(N_HEADS):
            s = s_next
            if hd + 1 < N_HEADS:
                s_next = scores(hd + 1)
            pair = slice((hd // 2) * LANES, (hd // 2 + 1) * LANES)
            m = jnp.max(s, axis=-1, keepdims=True)
            e = jnp.exp2(s - m).astype(BF16)
            pv = _dot(e, jnp.concatenate([v_ref[win, pair], ones], axis=1))
            l = pv[:, LANES:]
            o_h = pv[:, :LANES] / l
            st = jnp.where(lane == hd, m + jnp.log2(l), st)
            if hd % 2 == 0:
                o_even = o_h
            else:
                o_ref[rows, pair] = jnp.where(lane < HEAD_DIM, o_even, o_h)
        st_ref[rows, :] = st
        return carry

    lax.fori_loop(0, nblk, body, 0, unroll=min(UNROLL_A, nblk))


def _attn_a(a_dil, tab, nb, dil):
    n = SEQ // dil
    view = a_dil.reshape(nb, n, dil * A_WIDTH)
    o, st = pl.pallas_call(
        functools.partial(_attn_a_kernel, n=n),
        grid=(nb, dil),
        in_specs=[pl.BlockSpec((None, n, 2 * GROUP_WIDTH), lambda b, r: (b, 0, 2 * r)),
                  pl.BlockSpec((None, n, GROUP_WIDTH), lambda b, r: (b, 0, 4 * r + 2)),
                  pl.BlockSpec((None, n, GROUP_WIDTH), lambda b, r: (b, 0, 4 * r + 3)),
                  pl.BlockSpec(tab.shape, lambda b, r: (0, 0, 0, 0))],
        out_specs=[pl.BlockSpec((None, n, GROUP_WIDTH), lambda b, r: (b, 0, r)),
                   pl.BlockSpec((None, n, LANES), lambda b, r: (b, 0, r))],
        out_shape=[jax.ShapeDtypeStruct((nb, n, dil * GROUP_WIDTH), F32),
                   jax.ShapeDtypeStruct((nb, n, dil * LANES), F32)],
        compiler_params=_cparams(2),
        name=f"attn_a_d{dil}",
    )(view, view, view, tab)
    return o.reshape(nb * n, dil * GROUP_WIDTH), st.reshape(nb * n, dil * LANES)


def _flash(problems):
    n_chunks = SEQ // TK_DENSE
    ones = jnp.ones((ONES_ROWS, TK_DENSE), BF16)
    n_sub = TK_DENSE // SUB_DENSE
    state = [None] * len(problems)

    def scores(c):
        return [[_dot(k_at(c)[j * SUB_DENSE:(j + 1) * SUB_DENSE], q_t) for j in range(n_sub)]
                for q_t, k_at, _ in problems]

    s_cur = scores(0)
    for c in range(n_chunks):
        if c + 1 < n_chunks:
            s_next = scores(c + 1)
        for i, (subs, (_, _, v_t_at)) in enumerate(zip(s_cur, problems)):
            col_max = functools.reduce(jnp.maximum, [jnp.max(s, axis=0, keepdims=True) for s in subs])
            m_new = col_max if c == 0 else jnp.maximum(state[i][0], col_max)
            v_ext = jnp.concatenate([v_t_at(c), ones], axis=0)
            acc = None if c == 0 else jnp.exp2(state[i][0] - m_new) * state[i][1]
            for j, s in enumerate(subs):
                part = _dot(v_ext[:, j * SUB_DENSE:(j + 1) * SUB_DENSE], jnp.exp2(s - m_new).astype(BF16))
                acc = part if acc is None else acc + part
            state[i] = (m_new, acc)
        s_cur = s_next
    outs = []
    for (_, _, v_t_at), (_, acc) in zip(problems, state):
        dv = acc.shape[0] - ONES_ROWS
        outs.append(acc[:dv] / acc[dv:dv + 1])
    return outs


def _chunk(c):
    return pl.ds(c * TK_DENSE, TK_DENSE)


def _attn_b_kernel(qt_ref, k_ref, vt_ref, o_ref):
    problems = []
    for g in range(2):
        q_t = jnp.concatenate([qt_ref[(2 * g) * LANES:(2 * g + 1) * LANES, :],
                               qt_ref[(2 * g + 1) * LANES:(2 * g + 2) * LANES, :]], axis=1)
        problems.append((q_t,
                         lambda c, g=g: k_ref[_chunk(c), g * LANES:(g + 1) * LANES],
                         lambda c, g=g: vt_ref[g * HEAD_DIM:(g + 1) * HEAD_DIM, _chunk(c)]))
    outs = _flash(problems)
    o_t = jnp.concatenate([o[:, j * TQ_DENSE:(j + 1) * TQ_DENSE] for o in outs for j in range(2)], axis=0)
    o_ref[...] = o_t.T


def _attn_b(b_qt, b_k, b_vt, nb):
    nq = SEQ // TQ_DENSE
    o = pl.pallas_call(
        _attn_b_kernel,
        grid=(nb, nq),
        in_specs=[pl.BlockSpec((N_HEADS * LANES, TQ_DENSE), lambda b, i: (0, b * nq + i)),
                  pl.BlockSpec((None, SEQ, 2 * LANES), lambda b, i: (b, 0, 0)),
                  pl.BlockSpec((2 * HEAD_DIM, SEQ), lambda b, i: (0, b))],
        out_specs=pl.BlockSpec((None, TQ_DENSE, GROUP_WIDTH), lambda b, i: (b, i, 0)),
        out_shape=jax.ShapeDtypeStruct((nb, SEQ, GROUP_WIDTH), F32),
        compiler_params=_cparams(2),
        name="attn_b",
    )(b_qt, b_k.reshape(nb, SEQ, 2 * LANES), b_vt)
    return o.reshape(nb * SEQ, GROUP_WIDTH)


def _attn_d_kernel(qt_ref, k_ref, vt_ref, o_ref):
    problems = []
    for j in range(2):
        problems.append((qt_ref[j * LANES:(j + 1) * LANES, :],
                         lambda c, j=j: k_ref[_chunk(c), j * LANES:(j + 1) * LANES],
                         lambda c, j=j: vt_ref[j * D_V:(j + 1) * D_V, _chunk(c)]))
    o_ref[...] = jnp.concatenate(_flash(problems), axis=0).T


def _attn_d(d_qt, d_k, d_vt, nb):
    nq = SEQ // TQ_MLA
    o = pl.pallas_call(
        _attn_d_kernel,
        grid=(nb, 2, nq),
        in_specs=[pl.BlockSpec((2 * LANES, TQ_MLA), lambda b, p, i: (p, b * nq + i)),
                  pl.BlockSpec((None, SEQ, 2 * LANES), lambda b, p, i: (b, 0, p)),
                  pl.BlockSpec((2 * D_V, SEQ), lambda b, p, i: (p, b))],
        out_specs=pl.BlockSpec((None, TQ_MLA, LANES), lambda b, p, i: (b, i, p)),
        out_shape=jax.ShapeDtypeStruct((nb, SEQ, GROUP_WIDTH), F32),
        compiler_params=_cparams(3),
        name="attn_d",
    )(d_qt, d_k.reshape(nb, SEQ, N_HEADS * LANES), d_vt)
    return o.reshape(nb * SEQ, GROUP_WIDTH)


def _attn_c_kernel(qt_ref, k_ref, vt_ref, tab_ref, o_ref):
    i = pl.program_id(1)
    n_groups = GRID_ROWS // ROWS_C
    var = jnp.where(i == 0, 1, jnp.where(i == n_groups - 1, 2, 0))
    k0 = pl.multiple_of(jnp.clip(i * ROWS_C - NA_ROWS // 2, 0, GRID_ROWS - WIN_ROWS_C) * GRID_W,
                        ROWS_C * GRID_W)
    keys = pl.ds(k0, WIN_ROWS_C * GRID_W)
    ones = jnp.ones((ONES_ROWS, WIN_ROWS_C * GRID_W), BF16)

    def scores(hd):
        pair = slice((hd // 2) * LANES, (hd // 2 + 1) * LANES)
        return _dot(k_ref[keys, pair], qt_ref[hd * LANES:(hd + 1) * LANES, :]) + tab_ref[var, hd]

    outs = []
    s_next = scores(0)
    for hd in range(N_HEADS):
        s = s_next
        if hd + 1 < N_HEADS:
            s_next = scores(hd + 1)
        p = jnp.exp2(s - jnp.max(s, axis=0, keepdims=True)).astype(BF16)
        v_ext = jnp.concatenate([vt_ref[hd * HEAD_DIM:(hd + 1) * HEAD_DIM, keys], ones], axis=0)
        acc = _dot(v_ext, p)
        outs.append(acc[:HEAD_DIM] / acc[HEAD_DIM:HEAD_DIM + 1])
    o_ref[...] = jnp.concatenate(outs, axis=0).T


def _attn_c(c_qt, c_k, c_vt, tab, nb):
    tq = ROWS_C * GRID_W
    nq = SEQ // tq
    o = pl.pallas_call(
        _attn_c_kernel,
        grid=(nb, nq),
        in_specs=[pl.BlockSpec((N_HEADS * LANES, tq), lambda b, i: (0, b * nq + i)),
                  pl.BlockSpec((None, SEQ, GROUP_WIDTH), lambda b, i: (b, 0, 0)),
                  pl.BlockSpec((GROUP_WIDTH, SEQ), lambda b, i: (0, b)),
                  pl.BlockSpec(tab.shape, lambda b, i: (0, 0, 0, 0), pipeline_mode=pl.Buffered(1))],
        out_specs=pl.BlockSpec((None, tq, GROUP_WIDTH), lambda b, i: (b, i, 0)),
        out_shape=jax.ShapeDtypeStruct((nb, SEQ, GROUP_WIDTH), F32),
        compiler_params=_cparams(2),
        name="attn_c",
    )(c_qt, c_k.reshape(nb, SEQ, GROUP_WIDTH), c_vt, tab)
    return o.reshape(nb * SEQ, GROUP_WIDTH)


def _out_ffn_kernel(x_ref, oa1_ref, sa1_ref, oa2_ref, sa2_ref, oa3_ref, sa3_ref, ob_ref, oc_ref, od_ref,
                    expand_ref, og_ref, wout_ref, nf_ref, wg_ref, wu_ref, wd_ref, fin_ref, y_ref,
                    o_scr, s_scr, *, final):
    for j, (dil, o_ref, s_ref) in enumerate(((DIL_MID, oa2_ref, sa2_ref), (DIL_FAR, oa3_ref, sa3_ref))):
        for r in range(dil):
            dst = pl.ds(r, TM_FFN // dil, stride=dil)
            for s in range(GROUP_WIDTH // LANES):
                c0 = r * GROUP_WIDTH + s * LANES
                o_scr[2 * j + s, dst, :] = o_ref[:, c0:c0 + LANES]
            s_scr[j, dst, :] = s_ref[:, r * LANES:(r + 1) * LANES]
    oa = [oa1_ref[...]] + [jnp.concatenate([o_scr[2 * j], o_scr[2 * j + 1]], axis=1) for j in range(2)]

    lses = [sa1_ref[...], s_scr[0], s_scr[1]]
    mx = jnp.maximum(jnp.maximum(lses[0], lses[1]), lses[2])
    es = [jnp.exp2(s - mx) for s in lses]
    den = es[0] + es[1] + es[2]
    expand = expand_ref[...]

    def per_head(w):
        hi = w.astype(BF16)
        lo = (w - hi.astype(F32)).astype(BF16)
        return _dot(hi, expand) + _dot(lo, expand)

    o_a = per_head(es[0] / den) * oa[0] + per_head(es[1] / den) * oa[1] + per_head(es[2] / den) * oa[2]

    x = x_ref[...]
    groups = (o_a, ob_ref[...], oc_ref[...], od_ref[...])
    normed = [_rms(o, og_ref[:, g * GROUP_WIDTH:(g + 1) * GROUP_WIDTH]).astype(BF16) for g, o in enumerate(groups)]
    x1 = x + _dot(jnp.concatenate(normed, axis=1), wout_ref[...])

    h = _rms(x1, nf_ref[...]).astype(BF16)

    def gate_up(c):
        cols = slice(FF_BOUNDS[c], FF_BOUNDS[c + 1])
        gate = _dot(h, wg_ref[:, cols])
        up = _dot(h, wu_ref[:, cols])
        return (gate * (1.0 / (1.0 + jnp.exp(-gate))) * up).astype(BF16)

    n_ff = len(FF_BOUNDS) - 1
    act = gate_up(0)
    ff = None
    for c in range(n_ff):
        act_next = gate_up(c + 1) if c + 1 < n_ff else None
        part = _dot(act, wd_ref[FF_BOUNDS[c]:FF_BOUNDS[c + 1], :])
        ff = part if ff is None else ff + part
        act = act_next
    x2 = x1 + ff
    if final:
        x2 = _rms(x2, fin_ref[...])
    y_ref[...] = x2


def _out_ffn(x2d, attn_outs, lw, expand, final_norm, final):
    t = x2d.shape[0]
    row = lambda i: (i, 0)

    def resident(a):
        return pl.BlockSpec(a.shape, lambda i: (0, 0), pipeline_mode=pl.Buffered(1))

    (oa1, sa1), (oa2, sa2), (oa3, sa3), ob, oc, od = attn_outs
    wide = pl.BlockSpec((TM_FFN, GROUP_WIDTH), row)
    a_specs = [spec for _, d in DILATED_PATTERNS
               for spec in (pl.BlockSpec((TM_FFN // d, d * GROUP_WIDTH), row),
                            pl.BlockSpec((TM_FFN // d, d * LANES), row))]
    weights = (expand, lw["out_gain"], lw["w_out"], lw["norm_ffn"], lw["w_gate"], lw["w_up"], lw["w_down"],
               final_norm)
    return pl.pallas_call(
        functools.partial(_out_ffn_kernel, final=final),
        grid=(t // TM_FFN,),
        in_specs=[pl.BlockSpec((TM_FFN, D_MODEL), row)] + a_specs + [wide, wide, wide]
                 + [resident(w) for w in weights],
        out_specs=pl.BlockSpec((TM_FFN, D_MODEL), row),
        out_shape=jax.ShapeDtypeStruct((t, D_MODEL), F32),
        scratch_shapes=[pltpu.VMEM((2 * GROUP_WIDTH // LANES, TM_FFN, LANES), F32),
                        pltpu.VMEM((2, TM_FFN, LANES), F32)],
        compiler_params=_cparams(1),
        name="out_ffn",
    )(x2d, oa1, sa1, oa2, sa2, oa3, sa3, ob, oc, od, *weights)


def _rope_angles(pos, dim):
    inv = 1.0 / (ROPE_THETA ** (jnp.arange(0, dim, 2, dtype=F32) / dim))
    return pos.astype(F32)[:, None] * inv[None, :]


def _t5_bucket(rel):
    nb = T5_BUCKETS // 2
    max_exact = nb // 2
    n = jnp.abs(rel)
    n_f = jnp.maximum(n, max_exact).astype(F32)
    large = max_exact + (jnp.log(n_f / max_exact) / math.log(T5_MAX_DIST / max_exact)
                         * (nb - max_exact)).astype(jnp.int32)
    large = jnp.minimum(large, nb - 1)
    return jnp.where(rel > 0, nb, 0) + jnp.where(n < max_exact, n, large)


def _tables(t5_bias):
    t = jnp.arange(SEQ, dtype=jnp.int32)
    z16 = jnp.zeros((SEQ, 16), F32)

    def trio(parts):
        return jnp.stack([jnp.concatenate(p, axis=1) for p in parts])

    ang_r = _rope_angles(t // GRID_W, HEAD_DIM // 2)
    ang_c = _rope_angles(t % GRID_W, HEAD_DIM // 2)
    cr, sr, cc, sc = jnp.cos(ang_r), jnp.sin(ang_r), jnp.cos(ang_c), jnp.sin(ang_c)
    rope_b = trio(([cr, cr, cc, cc] * 2, [-sr, z16, -sc, z16] * 2, [z16, sr, z16, sc] * 2))

    ang_t = _rope_angles(t, D_ROPE)
    ct, st = jnp.cos(ang_t), jnp.sin(ang_t)
    ones64, z64, z32 = jnp.ones((SEQ, 64), F32), jnp.zeros((SEQ, 64), F32), jnp.zeros((SEQ, 32), F32)
    rope_d = trio(([ones64, ct, ct, z32], [z64, -st, z16, z32], [z64, z16, st, z32]))

    seg = jnp.kron(jnp.eye(2, dtype=F32), jnp.full((HEAD_DIM, HEAD_DIM), 1.0 / HEAD_DIM, F32)).astype(BF16)
    expand = jnp.zeros((LANES, GROUP_WIDTH), F32).at[:N_HEADS].set(
        jnp.kron(jnp.eye(N_HEADS, dtype=F32), jnp.ones((1, HEAD_DIM), F32))).astype(BF16)

    period = 4 * QBLK_A
    u = np.arange(period)
    rel = np.where(u < period - QBLK_A, u, u - period)
    band = []
    for _, dil in DILATED_PATTERNS:
        off = jnp.arange(-BAND_HALF, BAND_HALF + 1, dtype=jnp.int32) * dil
        bias = t5_bias[_t5_bucket(off)].T.astype(F32)
        variants = []
        for shift in (-BAND_HALF, 0, -2 * BAND_HALF):
            delta = rel + shift
            inside = np.abs(delta) <= BAND_HALF
            diag = jnp.where(inside[None], bias[:, np.clip(delta + BAND_HALF, 0, 2 * BAND_HALF)] * LOG2E,
                             NEG_INF)
            variants.append(_toeplitz(diag, QBLK_A, 2 * QBLK_A))
        band.append(jnp.stack(variants))
    return {"rope_b": rope_b, "rope_d": rope_d, "seg": seg, "expand": expand, "band": band}


def _toeplitz(w, n_rows, n_cols):
    period = w.shape[-1]
    tiled = jnp.tile(w, (1,) * (w.ndim - 1) + (n_rows,))
    return tiled[..., :n_rows * (period - 1)].reshape(w.shape[:-1] + (n_rows, period - 1))[..., :n_cols]


def _na_table(rpb):
    cols = np.arange(GRID_W)
    cs = np.clip(cols - NA_COLS // 2, 0, GRID_W - NA_COLS)
    inside = (cols[None, :] >= cs[:, None]) & (cols[None, :] < cs[:, None] + NA_COLS)
    period = 2 * GRID_W
    u = np.arange(period)
    dc = np.where(u < GRID_W, u, u - period) + NA_COLS - 1
    diag = jnp.where(((dc >= 0) & (dc <= 2 * NA_COLS - 2))[None, None],
                     rpb.astype(F32)[:, :, np.clip(dc, 0, 2 * NA_COLS - 2)], NEG_INF)
    vals = jnp.where(inside[None, None], _toeplitz(diag, GRID_W, GRID_W), NEG_INF)
    vals = jnp.swapaxes(vals, 2, 3) * LOG2E
    masked = jnp.full((N_HEADS, GRID_W, GRID_W), NEG_INF, F32)
    tabs = []
    for lo, step, base in ((0, 1, NA_ROWS // 2 - 1), (0, 0, NA_ROWS - 1), (NA_ROWS // 2, 0, -1)):
        rows = []
        for w in range(WIN_ROWS_C):
            per_a = [vals[:, w - a + base] if lo <= w - a * step < lo + NA_ROWS else masked
                     for a in range(ROWS_C)]
            rows.append(jnp.concatenate(per_a, axis=-1))
        tabs.append(jnp.concatenate(rows, axis=1))
    return jnp.stack(tabs)


def _layer_weights(l, norm_mix, w_in, b_q_gain, b_k_gain, d_q_gain, d_w_uq, d_kv_gain, d_w_ukv,
                   out_gain, w_out, norm_ffn, w_gate, w_up, w_down):
    d_in = w_in.shape[-1]
    kr_lo = d_in - D_ROPE
    w_in_p = jnp.zeros((D_MODEL, D_IN_PAD), F32)
    w_in_p = w_in_p.at[:, :kr_lo].set(w_in[l][:, :kr_lo])
    w_in_p = w_in_p.at[:, kr_lo + D_NOPE:kr_lo + D_NOPE + D_ROPE].set(w_in[l][:, kr_lo:])
    dqk = D_NOPE + D_ROPE
    w_uq = jnp.zeros((D_Q_LORA, N_HEADS, LANES), F32).at[:, :, :dqk].set(
        d_w_uq[l].reshape(D_Q_LORA, N_HEADS, dqk)).reshape(D_Q_LORA, N_HEADS * LANES)
    ukv = d_w_ukv[l].reshape(D_KV_LORA, N_HEADS, D_NOPE + D_V)
    w_uk = jnp.zeros((D_KV_LORA, N_HEADS, LANES), F32).at[:, :, :D_NOPE].set(
        ukv[:, :, :D_NOPE]).reshape(D_KV_LORA, N_HEADS * LANES)
    w_uv = ukv[:, :, D_NOPE:].reshape(D_KV_LORA, N_HEADS * D_V)
    return {
        "norm_mix": norm_mix[l][None, :], "w_in": w_in_p.astype(BF16),
        "b_q_gain": jnp.tile(b_q_gain[l], 2)[None, :], "b_k_gain": jnp.tile(b_k_gain[l], 2)[None, :],
        "d_q_gain": d_q_gain[l][None, :], "w_uq": w_uq.astype(BF16),
        "d_kv_gain": d_kv_gain[l][None, :], "w_uk": w_uk.astype(BF16), "w_uv": w_uv.astype(BF16),
        "out_gain": out_gain[l][None, :], "w_out": w_out[l].astype(BF16), "norm_ffn": norm_ffn[l][None, :],
        "w_gate": w_gate[l].astype(BF16), "w_up": w_up[l].astype(BF16), "w_down": w_down[l].astype(BF16),
    }


def _trunk(x, layers, tabs, na_tabs, final_norm):
    nb = x.shape[0]
    x2d = x.reshape(nb * SEQ, D_MODEL)
    for l, lw in enumerate(layers):
        a1, a4, a16, b_qt, b_k, b_vt, c_qt, c_k, c_vt, d_qt, d_k, d_vt = _proj(x2d, lw, tabs)
        outs_a = tuple(_attn_a(a_dil, tabs["band"][p], nb, dil)
                       for p, (a_dil, (_, dil)) in enumerate(zip((a1, a4, a16), DILATED_PATTERNS)))
        o_b = _attn_b(b_qt, b_k, b_vt, nb)
        o_c = _attn_c(c_qt, c_k, c_vt, na_tabs[l], nb)
        o_d = _attn_d(d_qt, d_k, d_vt, nb)
        x2d = _out_ffn(x2d, outs_a + (o_b, o_c, o_d), lw, tabs["expand"], final_norm, final=(l == DEPTH - 1))
    return x2d.reshape(nb, SEQ, D_MODEL)


def kernel(x_prompt, x_sample, t5_bias, norm_mix, w_in, b_q_gain, b_k_gain, c_rpb, d_q_gain, d_w_uq,
           d_kv_gain, d_w_ukv, out_gain, w_out, norm_ffn, w_gate, w_up, w_down, final_norm):
    tabs = _tables(t5_bias)
    na_tabs = [_na_table(c_rpb[l]) for l in range(DEPTH)]
    layers = [_layer_weights(l, norm_mix, w_in, b_q_gain, b_k_gain, d_q_gain, d_w_uq, d_kv_gain, d_w_ukv,
                             out_gain, w_out, norm_ffn, w_gate, w_up, w_down) for l in range(DEPTH)]
    fin = final_norm[None, :]
    return (_trunk(x_prompt, layers, tabs, na_tabs, fin), _trunk(x_sample, layers, tabs, na_tabs, fin))
```

```python
import functools
import math

import jax
import jax.numpy as jnp
import numpy as np
from jax import lax
from jax.experimental import pallas as pl
from jax.experimental.pallas import tpu as pltpu

F32 = jnp.float32
BF16 = jnp.bfloat16

D_MODEL = 1024
SEQ = 4096
DEPTH = 2
HEAD_DIM = 64
N_HEADS = 4
GROUP_WIDTH = N_HEADS * HEAD_DIM
DILATED_PATTERNS = ((128, 1), (512, 4), (2048, 16))
BAND_HALF = 64
DIL_MID, DIL_FAR = DILATED_PATTERNS[1][1], DILATED_PATTERNS[2][1]
T5_BUCKETS = 32
T5_MAX_DIST = 1024
GRID_W = 64
GRID_ROWS = SEQ // GRID_W
NA_ROWS = 8
NA_COLS = 16
D_Q_LORA = 256
D_KV_LORA = 128
D_NOPE = 64
D_ROPE = 32
D_V = 64
ROPE_THETA = 10000.0
D_FF = 2816
RMS_EPS = 1e-6
NEG_INF = -1e30

LANES = 128
D_IN_PAD = 2560
COL_A, COL_B, COL_C, COL_D = 0, 768, 1280, 2048
A_WIDTH = 1024
ATTN_SCALE = HEAD_DIM ** -0.5
MLA_SCALE = (D_NOPE + D_ROPE) ** -0.5
LOG2E = math.log2(math.e)

TM_PROJ = 512
PROJ_SPLIT = 2
QBLK_A = 128
UNROLL_A = 4
TQ_DENSE = 256
TQ_MLA = 256
TK_DENSE = 512
SUB_DENSE = 256
ONES_ROWS = 16
ROWS_C = 4
WIN_ROWS_C = 12
TM_FFN = 512
FF_BOUNDS = (0, 768, 1536, 2304, D_FF)
VMEM_LIMIT = 56 * 1024 * 1024


def _cparams(n_axes):
    return pltpu.CompilerParams(dimension_semantics=("arbitrary",) * n_axes,
                                vmem_limit_bytes=VMEM_LIMIT)


def _rms(x, g):
    return x * lax.rsqrt(jnp.mean(x * x, axis=-1, keepdims=True) + RMS_EPS) * g


def _dot(a, b):
    return jnp.dot(a, b, preferred_element_type=F32)


def _dot_nt(a, b):
    return lax.dot_general(a, b, (((1,), (1,)), ((), ())), preferred_element_type=F32)


def _rope(y, tab):
    return y * tab[0] + pltpu.roll(y, LANES - 16, 1) * tab[1] + pltpu.roll(y, 16, 1) * tab[2]


def _proj_kernel(x_ref, gmix_ref, w_ref, bqg_ref, bkg_ref, seg_ref, tabb_ref, tabd_ref,
                 dqg_ref, wuq_ref, dkvg_ref, wuk_ref, wuv_ref,
                 a_ref, a4_ref, a16_ref, bqt_ref, bk_ref, bvt_ref, cqt_ref, ck_ref, cvt_ref, dqt_ref, dk_ref,
                 dvt_ref, a_scr, a4_scr):
    half = TM_PROJ // PROJ_SPLIT
    low_half = lax.broadcasted_iota(jnp.int32, (half, LANES), 1) < HEAD_DIM
    zeros_half = jnp.zeros((HEAD_DIM, half), BF16)
    seg = seg_ref[...]

    def main(rows):
        h = _rms(x_ref[rows, :], gmix_ref[...]).astype(BF16)
        return [_dot(h, w_ref[:, lo:hi]) for lo, hi in ((COL_A, COL_B), (COL_B, COL_C), (COL_C, COL_D),
                                                        (COL_D, D_IN_PAD))]

    def finish(rows, pa, pb, pc, pd):
        slabs = []
        for hd in range(N_HEADS):
            pair = pa[:, (hd // 2) * LANES:(hd // 2 + 1) * LANES] * (ATTN_SCALE * LOG2E)
            own = low_half if hd % 2 == 0 else jnp.logical_not(low_half)
            slabs.append(jnp.where(own, pair, 0.0))
        slabs += [pa[:, GROUP_WIDTH + j * LANES:GROUP_WIDTH + (j + 1) * LANES] for j in range(4)]
        for s, val in enumerate(slabs):
            a_scr[s, rows, :] = val
            a_ref[rows, s * LANES:(s + 1) * LANES] = val.astype(BF16)

        cqt = (pc[:, :GROUP_WIDTH] * (ATTN_SCALE * LOG2E)).T.astype(BF16)
        for hd in range(N_HEADS):
            own = hd * LANES + (hd % 2) * HEAD_DIM
            other = hd * LANES + (1 - hd % 2) * HEAD_DIM
            cqt_ref[own:own + HEAD_DIM, rows] = cqt[hd * HEAD_DIM:(hd + 1) * HEAD_DIM]
            cqt_ref[other:other + HEAD_DIM, rows] = zeros_half
        ck_ref[rows, :] = pc[:, GROUP_WIDTH:2 * GROUP_WIDTH].astype(BF16)
        cvt_ref[:, rows] = pc[:, 2 * GROUP_WIDTH:].T.astype(BF16)

        tabb = tabb_ref[:, rows, :]

        def head_norm_rope(chunk, gain):
            sq = chunk * chunk
            hi = sq.astype(BF16)
            lo = (sq - hi.astype(F32)).astype(BF16)
            ms = _dot(hi, seg) + _dot(lo, seg)
            return _rope(chunk * lax.rsqrt(ms + RMS_EPS) * gain, tabb)

        for c in range(2):
            q = head_norm_rope(pb[:, c * LANES:(c + 1) * LANES], bqg_ref[...])
            qt = (q * (ATTN_SCALE * LOG2E)).T.astype(BF16)
            for j in range(2):
                r0 = (2 * c + j) * LANES
                bqt_ref[r0:r0 + HEAD_DIM, rows] = qt[j * HEAD_DIM:(j + 1) * HEAD_DIM]
                bqt_ref[r0 + HEAD_DIM:r0 + LANES, rows] = zeros_half
        kb = head_norm_rope(pb[:, 2 * LANES:3 * LANES], bkg_ref[...])
        bk_ref[rows, :LANES] = kb.astype(BF16)
        bk_ref[rows, LANES:] = pltpu.roll(kb, HEAD_DIM, 1).astype(BF16)
        bvt_ref[:, rows] = pb[:, 3 * LANES:].T.astype(BF16)

        tabd = tabd_ref[:, rows, :]
        dq = _rms(pd[:, :D_Q_LORA], dqg_ref[...]).astype(BF16)
        qd = _dot(dq, wuq_ref[...])
        dkv = _rms(pd[:, D_Q_LORA:D_Q_LORA + D_KV_LORA], dkvg_ref[...]).astype(BF16)
        kn = _dot(dkv, wuk_ref[...])
        dvt_ref[:, rows] = _dot(dkv, wuv_ref[...]).T.astype(BF16)
        kr = _rope(pd[:, D_Q_LORA + D_KV_LORA:], tabd)
        for hd in range(N_HEADS):
            sl = slice(hd * LANES, (hd + 1) * LANES)
            dqt_ref[sl, rows] = (_rope(qd[:, sl], tabd) * (MLA_SCALE * LOG2E)).T.astype(BF16)
            dk_ref[rows, sl] = (kn[:, sl] + kr).astype(BF16)

    row_sets = [slice(j * half, (j + 1) * half) for j in range(PROJ_SPLIT)]
    mains = [main(rows) for rows in row_sets]
    for rows, (pa, pb, pc, pd) in zip(row_sets, mains):
        finish(rows, pa, pb, pc, pd)

    n_slabs = A_WIDTH // LANES
    for r in range(DIL_MID):
        for s in range(n_slabs):
            val = a_scr[s, pl.ds(r, TM_PROJ // DIL_MID, stride=DIL_MID), :]
            a4_scr[r * n_slabs + s] = val
            a4_ref[:, (r * n_slabs + s) * LANES:(r * n_slabs + s + 1) * LANES] = val.astype(BF16)
    step = DIL_FAR // DIL_MID
    for j in range(step):
        for s in range(DIL_MID * n_slabs):
            c0 = (j * DIL_MID * n_slabs + s) * LANES
            a16_ref[:, c0:c0 + LANES] = a4_scr[s, pl.ds(j, TM_PROJ // DIL_FAR, stride=step), :].astype(BF16)


def _proj(x2d, lw, tabs):
    t = x2d.shape[0]
    blocks_per_seq = SEQ // TM_PROJ
    row = lambda i: (i, 0)
    const2 = lambda i: (0, 0)
    pos3 = lambda i: (0, i % blocks_per_seq, 0)

    def full(a):
        return pl.BlockSpec(a.shape, const2)

    col = lambda i: (0, i)
    outs = ((512, True), (256, False), (128, True), (512, True), (256, False), (256, True),
            (512, True), (512, False), (256, True))
    a_specs = [pl.BlockSpec((TM_PROJ // d, d * A_WIDTH), row) for _, d in DILATED_PATTERNS]
    a_shapes = [jax.ShapeDtypeStruct((t // d, d * A_WIDTH), BF16) for _, d in DILATED_PATTERNS]
    return pl.pallas_call(
        _proj_kernel,
        grid=(t // TM_PROJ,),
        scratch_shapes=[pltpu.VMEM((A_WIDTH // LANES, TM_PROJ, LANES), F32),
                        pltpu.VMEM((DIL_MID * A_WIDTH // LANES, TM_PROJ // DIL_MID, LANES), F32)],
        in_specs=[pl.BlockSpec((TM_PROJ, D_MODEL), row), full(lw["norm_mix"]), full(lw["w_in"]),
                  full(lw["b_q_gain"]), full(lw["b_k_gain"]), full(tabs["seg"]),
                  pl.BlockSpec((3, TM_PROJ, LANES), pos3), pl.BlockSpec((3, TM_PROJ, LANES), pos3),
                  full(lw["d_q_gain"]), full(lw["w_uq"]), full(lw["d_kv_gain"]), full(lw["w_uk"]),
                  full(lw["w_uv"])],
        out_specs=a_specs + [pl.BlockSpec((w, TM_PROJ), col) if tr else pl.BlockSpec((TM_PROJ, w), row)
                             for w, tr in outs],
        out_shape=a_shapes + [jax.ShapeDtypeStruct((w, t) if tr else (t, w), BF16) for w, tr in outs],
        compiler_params=_cparams(1),
        name="proj",
    )(x2d, lw["norm_mix"], lw["w_in"], lw["b_q_gain"], lw["b_k_gain"], tabs["seg"], tabs["rope_b"],
      tabs["rope_d"], lw["d_q_gain"], lw["w_uq"], lw["d_kv_gain"], lw["w_uk"], lw["w_uv"])


def _attn_a_kernel(q_ref, k_ref, v_ref, tab_ref, o_ref, st_ref, *, n):
    nblk = n // QBLK_A
    lane = lax.broadcasted_iota(jnp.int32, (QBLK_A, LANES), 1)
    ones = jnp.ones((2 * QBLK_A, LANES), BF16)

    def body(i, carry):
        rows = pl.ds(pl.multiple_of(i * QBLK_A, QBLK_A), QBLK_A)
        win = pl.ds(pl.multiple_of(jnp.clip(i * QBLK_A - BAND_HALF, 0, n - 2 * QBLK_A), BAND_HALF), 2 * QBLK_A)
        var = jnp.where(i == 0, 1, jnp.where(i == nblk - 1, 2, 0))

        def scores(hd):
            pair = slice((hd // 2) * LANES, (hd // 2 + 1) * LANES)
            return _dot_nt(q_ref[rows, hd * LANES:(hd + 1) * LANES], k_ref[win, pair]) + tab_ref[var, hd]

        st = jnp.zeros((QBLK_A, LANES), F32)
        s_next = scores(0)
        for hd in range(N_HEADS):
            s = s_next
            if hd + 1 < N_HEADS:
                s_next = scores(hd + 1)
            pair = slice((hd // 2) * LANES, (hd // 2 + 1) * LANES)
            m = jnp.max(s, axis=-1, keepdims=True)
            e = jnp.exp2(s - m).astype(BF16)
            pv = _dot(e, jnp.concatenate([v_ref[win, pair], ones], axis=1))
            l = pv[:, LANES:]
            o_h = pv[:, :LANES] / l
            st = jnp.where(lane == hd, m + jnp.log2(l), st)
            if hd % 2 == 0:
                o_even = o_h
            else:
                o_ref[rows, pair] = jnp.where(lane < HEAD_DIM, o_even, o_h)
        st_ref[rows, :] = st
        return carry

    lax.fori_loop(0, nblk, body, 0, unroll=min(UNROLL_A, nblk))


def _attn_a(a_dil, tab, nb, dil):
    n = SEQ // dil
    view = a_dil.reshape(nb, n, dil * A_WIDTH)
    o, st = pl.pallas_call(
        functools.partial(_attn_a_kernel, n=n),
        grid=(nb, dil),
        in_specs=[pl.BlockSpec((None, n, 2 * GROUP_WIDTH), lambda b, r: (b, 0, 2 * r)),
                  pl.BlockSpec((None, n, GROUP_WIDTH), lambda b, r: (b, 0, 4 * r + 2)),
                  pl.BlockSpec((None, n, GROUP_WIDTH), lambda b, r: (b, 0, 4 * r + 3)),
                  pl.BlockSpec(tab.shape, lambda b, r: (0, 0, 0, 0))],
        out_specs=[pl.BlockSpec((None, n, GROUP_WIDTH), lambda b, r: (b, 0, r)),
                   pl.BlockSpec((None, n, LANES), lambda b, r: (b, 0, r))],
        out_shape=[jax.ShapeDtypeStruct((nb, n, dil * GROUP_WIDTH), F32),
                   jax.ShapeDtypeStruct((nb, n, dil * LANES), F32)],
        compiler_params=_cparams(2),
        name=f"attn_a_d{dil}",
    )(view, view, view, tab)
    return o.reshape(nb * n, dil * GROUP_WIDTH), st.reshape(nb * n, dil * LANES)


def _flash(problems):
    n_chunks = SEQ // TK_DENSE
    ones = jnp.ones((ONES_ROWS, TK_DENSE), BF16)
    n_sub = TK_DENSE // SUB_DENSE
    state = [None] * len(problems)

    def scores(c):
        return [[_dot(k_at(c)[j * SUB_DENSE:(j + 1) * SUB_DENSE], q_t) for j in range(n_sub)]
                for q_t, k_at, _ in problems]

    s_cur = scores(0)
    for c in range(n_chunks):
        if c + 1 < n_chunks:
            s_next = scores(c + 1)
        for i, (subs, (_, _, v_t_at)) in enumerate(zip(s_cur, problems)):
            col_max = functools.reduce(jnp.maximum, [jnp.max(s, axis=0, keepdims=True) for s in subs])
            m_new = col_max if c == 0 else jnp.maximum(state[i][0], col_max)
            v_ext = jnp.concatenate([v_t_at(c), ones], axis=0)
            acc = None if c == 0 else jnp.exp2(state[i][0] - m_new) * state[i][1]
            for j, s in enumerate(subs):
                part = _dot(v_ext[:, j * SUB_DENSE:(j + 1) * SUB_DENSE], jnp.exp2(s - m_new).astype(BF16))
                acc = part if acc is None else acc + part
            state[i] = (m_new, acc)
        s_cur = s_next
    outs = []
    for (_, _, v_t_at), (_, acc) in zip(problems, state):
        dv = acc.shape[0] - ONES_ROWS
        outs.append(acc[:dv] / acc[dv:dv + 1])
    return outs


def _chunk(c):
    return pl.ds(c * TK_DENSE, TK_DENSE)


def _attn_b_kernel(qt_ref, k_ref, vt_ref, o_ref):
    problems = []
    for hd in range(N_HEADS):
        g = hd // 2
        problems.append((qt_ref[hd * LANES:(hd + 1) * LANES, :],
                         lambda c, g=g: k_ref[_chunk(c), g * LANES:(g + 1) * LANES],
                         lambda c, g=g: vt_ref[g * HEAD_DIM:(g + 1) * HEAD_DIM, _chunk(c)]))
    o_ref[...] = jnp.concatenate(_flash(problems), axis=0).T


def _attn_b(b_qt, b_k, b_vt, nb):
    nq = SEQ // TQ_DENSE
    o = pl.pallas_call(
        _attn_b_kernel,
        grid=(nb, nq),
        in_specs=[pl.BlockSpec((N_HEADS * LANES, TQ_DENSE), lambda b, i: (0, b * nq + i)),
                  pl.BlockSpec((None, SEQ, 2 * LANES), lambda b, i: (b, 0, 0)),
                  pl.BlockSpec((2 * HEAD_DIM, SEQ), lambda b, i: (0, b))],
        out_specs=pl.BlockSpec((None, TQ_DENSE, GROUP_WIDTH), lambda b, i: (b, i, 0)),
        out_shape=jax.ShapeDtypeStruct((nb, SEQ, GROUP_WIDTH), F32),
        compiler_params=_cparams(2),
        name="attn_b",
    )(b_qt, b_k.reshape(nb, SEQ, 2 * LANES), b_vt)
    return o.reshape(nb * SEQ, GROUP_WIDTH)


def _attn_d_kernel(qt_ref, k_ref, vt_ref, o_ref):
    problems = []
    for j in range(N_HEADS):
        problems.append((qt_ref[j * LANES:(j + 1) * LANES, :],
                         lambda c, j=j: k_ref[_chunk(c), j * LANES:(j + 1) * LANES],
                         lambda c, j=j: vt_ref[j * D_V:(j + 1) * D_V, _chunk(c)]))
    o_ref[...] = jnp.concatenate(_flash(problems), axis=0).T


def _attn_d(d_qt, d_k, d_vt, nb):
    nq = SEQ // TQ_MLA
    o = pl.pallas_call(
        _attn_d_kernel,
        grid=(nb, nq),
        in_specs=[pl.BlockSpec((N_HEADS * LANES, TQ_MLA), lambda b, i: (0, b * nq + i)),
                  pl.BlockSpec((None, SEQ, N_HEADS * LANES), lambda b, i: (b, 0, 0)),
                  pl.BlockSpec((N_HEADS * D_V, SEQ), lambda b, i: (0, b))],
        out_specs=pl.BlockSpec((None, TQ_MLA, GROUP_WIDTH), lambda b, i: (b, i, 0)),
        out_shape=jax.ShapeDtypeStruct((nb, SEQ, GROUP_WIDTH), F32),
        compiler_params=_cparams(2),
        name="attn_d",
    )(d_qt, d_k.reshape(nb, SEQ, N_HEADS * LANES), d_vt)
    return o.reshape(nb * SEQ, GROUP_WIDTH)


def _attn_c_kernel(qt_ref, k_ref, vt_ref, tab_ref, o_ref):
    i = pl.program_id(1)
    n_groups = GRID_ROWS // ROWS_C
    var = jnp.where(i == 0, 1, jnp.where(i == n_groups - 1, 2, 0))
    k0 = pl.multiple_of(jnp.clip(i * ROWS_C - NA_ROWS // 2, 0, GRID_ROWS - WIN_ROWS_C) * GRID_W,
                        ROWS_C * GRID_W)
    keys = pl.ds(k0, WIN_ROWS_C * GRID_W)
    ones = jnp.ones((ONES_ROWS, WIN_ROWS_C * GRID_W), BF16)

    def scores(hd):
        pair = slice((hd // 2) * LANES, (hd // 2 + 1) * LANES)
        return _dot(k_ref[keys, pair], qt_ref[hd * LANES:(hd + 1) * LANES, :]) + tab_ref[var, hd]

    outs = []
    s_next = scores(0)
    for hd in range(N_HEADS):
        s = s_next
        if hd + 1 < N_HEADS:
            s_next = scores(hd + 1)
        p = jnp.exp2(s - jnp.max(s, axis=0, keepdims=True)).astype(BF16)
        v_ext = jnp.concatenate([vt_ref[hd * HEAD_DIM:(hd + 1) * HEAD_DIM, keys], ones], axis=0)
        acc = _dot(v_ext, p)
        outs.append(acc[:HEAD_DIM] / acc[HEAD_DIM:HEAD_DIM + 1])
    o_ref[...] = jnp.concatenate(outs, axis=0).T


def _attn_c(c_qt, c_k, c_vt, tab, nb):
    tq = ROWS_C * GRID_W
    nq = SEQ // tq
    o = pl.pallas_call(
        _attn_c_kernel,
        grid=(nb, nq),
        in_specs=[pl.BlockSpec((N_HEADS * LANES, tq), lambda b, i: (0, b * nq + i)),
                  pl.BlockSpec((None, SEQ, GROUP_WIDTH), lambda b, i: (b, 0, 0)),
                  pl.BlockSpec((GROUP_WIDTH, SEQ), lambda b, i: (0, b)),
                  pl.BlockSpec(tab.shape, lambda b, i: (0, 0, 0, 0), pipeline_mode=pl.Buffered(1))],
        out_specs=pl.BlockSpec((None, tq, GROUP_WIDTH), lambda b, i: (b, i, 0)),
        out_shape=jax.ShapeDtypeStruct((nb, SEQ, GROUP_WIDTH), F32),
        compiler_params=_cparams(2),
        name="attn_c",
    )(c_qt, c_k.reshape(nb, SEQ, GROUP_WIDTH), c_vt, tab)
    return o.reshape(nb * SEQ, GROUP_WIDTH)


def _out_ffn_kernel(x_ref, oa1_ref, sa1_ref, oa2_ref, sa2_ref, oa3_ref, sa3_ref, ob_ref, oc_ref, od_ref,
                    expand_ref, og_ref, wout_ref, nf_ref, wg_ref, wu_ref, wd_ref, fin_ref, y_ref,
                    o_scr, s_scr, *, final):
    for j, (dil, o_ref, s_ref) in enumerate(((DIL_MID, oa2_ref, sa2_ref), (DIL_FAR, oa3_ref, sa3_ref))):
        for r in range(dil):
            dst = pl.ds(r, TM_FFN // dil, stride=dil)
            for s in range(GROUP_WIDTH // LANES):
                c0 = r * GROUP_WIDTH + s * LANES
                o_scr[2 * j + s, dst, :] = o_ref[:, c0:c0 + LANES]
            s_scr[j, dst, :] = s_ref[:, r * LANES:(r + 1) * LANES]
    oa = [oa1_ref[...]] + [jnp.concatenate([o_scr[2 * j], o_scr[2 * j + 1]], axis=1) for j in range(2)]

    lses = [sa1_ref[...], s_scr[0], s_scr[1]]
    mx = jnp.maximum(jnp.maximum(lses[0], lses[1]), lses[2])
    es = [jnp.exp2(s - mx) for s in lses]
    den = es[0] + es[1] + es[2]
    expand = expand_ref[...]

    def per_head(w):
        hi = w.astype(BF16)
        lo = (w - hi.astype(F32)).astype(BF16)
        return _dot(hi, expand) + _dot(lo, expand)

    o_a = per_head(es[0] / den) * oa[0] + per_head(es[1] / den) * oa[1] + per_head(es[2] / den) * oa[2]

    x = x_ref[...]
    groups = (o_a, ob_ref[...], oc_ref[...], od_ref[...])
    normed = [_rms(o, og_ref[:, g * GROUP_WIDTH:(g + 1) * GROUP_WIDTH]).astype(BF16) for g, o in enumerate(groups)]
    x1 = x + _dot(jnp.concatenate(normed, axis=1), wout_ref[...])

    h = _rms(x1, nf_ref[...]).astype(BF16)

    def gate_up(c):
        cols = slice(FF_BOUNDS[c], FF_BOUNDS[c + 1])
        gate = _dot(h, wg_ref[:, cols])
        up = _dot(h, wu_ref[:, cols])
        return (gate * (1.0 / (1.0 + jnp.exp(-gate))) * up).astype(BF16)

    n_ff = len(FF_BOUNDS) - 1
    act = gate_up(0)
    ff = None
    for c in range(n_ff):
        act_next = gate_up(c + 1) if c + 1 < n_ff else None
        part = _dot(act, wd_ref[FF_BOUNDS[c]:FF_BOUNDS[c + 1], :])
        ff = part if ff is None else ff + part
        act = act_next
    x2 = x1 + ff
    if final:
        x2 = _rms(x2, fin_ref[...])
    y_ref[...] = x2


def _out_ffn(x2d, attn_outs, lw, expand, final_norm, final):
    t = x2d.shape[0]
    row = lambda i: (i, 0)

    def resident(a):
        return pl.BlockSpec(a.shape, lambda i: (0, 0), pipeline_mode=pl.Buffered(1))

    (oa1, sa1), (oa2, sa2), (oa3, sa3), ob, oc, od = attn_outs
    wide = pl.BlockSpec((TM_FFN, GROUP_WIDTH), row)
    a_specs = [spec for _, d in DILATED_PATTERNS
               for spec in (pl.BlockSpec((TM_FFN // d, d * GROUP_WIDTH), row),
                            pl.BlockSpec((TM_FFN // d, d * LANES), row))]
    weights = (expand, lw["out_gain"], lw["w_out"], lw["norm_ffn"], lw["w_gate"], lw["w_up"], lw["w_down"],
               final_norm)
    return pl.pallas_call(
        functools.partial(_out_ffn_kernel, final=final),
        grid=(t // TM_FFN,),
        in_specs=[pl.BlockSpec((TM_FFN, D_MODEL), row)] + a_specs + [wide, wide, wide]
                 + [resident(w) for w in weights],
        out_specs=pl.BlockSpec((TM_FFN, D_MODEL), row),
        out_shape=jax.ShapeDtypeStruct((t, D_MODEL), F32),
        scratch_shapes=[pltpu.VMEM((2 * GROUP_WIDTH // LANES, TM_FFN, LANES), F32),
                        pltpu.VMEM((2, TM_FFN, LANES), F32)],
        compiler_params=_cparams(1),
        name="out_ffn",
    )(x2d, oa1, sa1, oa2, sa2, oa3, sa3, ob, oc, od, *weights)


def _rope_angles(pos, dim):
    inv = 1.0 / (ROPE_THETA ** (jnp.arange(0, dim, 2, dtype=F32) / dim))
    return pos.astype(F32)[:, None] * inv[None, :]


def _t5_bucket(rel):
    nb = T5_BUCKETS // 2
    max_exact = nb // 2
    n = jnp.abs(rel)
    n_f = jnp.maximum(n, max_exact).astype(F32)
    large = max_exact + (jnp.log(n_f / max_exact) / math.log(T5_MAX_DIST / max_exact)
                         * (nb - max_exact)).astype(jnp.int32)
    large = jnp.minimum(large, nb - 1)
    return jnp.where(rel > 0, nb, 0) + jnp.where(n < max_exact, n, large)


def _tables(t5_bias):
    t = jnp.arange(SEQ, dtype=jnp.int32)
    z16 = jnp.zeros((SEQ, 16), F32)

    def trio(parts):
        return jnp.stack([jnp.concatenate(p, axis=1) for p in parts])

    ang_r = _rope_angles(t // GRID_W, HEAD_DIM // 2)
    ang_c = _rope_angles(t % GRID_W, HEAD_DIM // 2)
    cr, sr, cc, sc = jnp.cos(ang_r), jnp.sin(ang_r), jnp.cos(ang_c), jnp.sin(ang_c)
    rope_b = trio(([cr, cr, cc, cc] * 2, [-sr, z16, -sc, z16] * 2, [z16, sr, z16, sc] * 2))

    ang_t = _rope_angles(t, D_ROPE)
    ct, st = jnp.cos(ang_t), jnp.sin(ang_t)
    ones64, z64, z32 = jnp.ones((SEQ, 64), F32), jnp.zeros((SEQ, 64), F32), jnp.zeros((SEQ, 32), F32)
    rope_d = trio(([ones64, ct, ct, z32], [z64, -st, z16, z32], [z64, z16, st, z32]))

    seg = jnp.kron(jnp.eye(2, dtype=F32), jnp.full((HEAD_DIM, HEAD_DIM), 1.0 / HEAD_DIM, F32)).astype(BF16)
    expand = jnp.zeros((LANES, GROUP_WIDTH), F32).at[:N_HEADS].set(
        jnp.kron(jnp.eye(N_HEADS, dtype=F32), jnp.ones((1, HEAD_DIM), F32))).astype(BF16)

    period = 4 * QBLK_A
    u = np.arange(period)
    rel = np.where(u < period - QBLK_A, u, u - period)
    band = []
    for _, dil in DILATED_PATTERNS:
        off = jnp.arange(-BAND_HALF, BAND_HALF + 1, dtype=jnp.int32) * dil
        bias = t5_bias[_t5_bucket(off)].T.astype(F32)
        variants = []
        for shift in (-BAND_HALF, 0, -2 * BAND_HALF):
            delta = rel + shift
            inside = np.abs(delta) <= BAND_HALF
            diag = jnp.where(inside[None], bias[:, np.clip(delta + BAND_HALF, 0, 2 * BAND_HALF)] * LOG2E,
                             NEG_INF)
            variants.append(_toeplitz(diag, QBLK_A, 2 * QBLK_A))
        band.append(jnp.stack(variants))
    return {"rope_b": rope_b, "rope_d": rope_d, "seg": seg, "expand": expand, "band": band}


def _toeplitz(w, n_rows, n_cols):
    period = w.shape[-1]
    tiled = jnp.tile(w, (1,) * (w.ndim - 1) + (n_rows,))
    return tiled[..., :n_rows * (period - 1)].reshape(w.shape[:-1] + (n_rows, period - 1))[..., :n_cols]


def _na_table(rpb):
    cols = np.arange(GRID_W)
    cs = np.clip(cols - NA_COLS // 2, 0, GRID_W - NA_COLS)
    inside = (cols[None, :] >= cs[:, None]) & (cols[None, :] < cs[:, None] + NA_COLS)
    period = 2 * GRID_W
    u = np.arange(period)
    dc = np.where(u < GRID_W, u, u - period) + NA_COLS - 1
    diag = jnp.where(((dc >= 0) & (dc <= 2 * NA_COLS - 2))[None, None],
                     rpb.astype(F32)[:, :, np.clip(dc, 0, 2 * NA_COLS - 2)], NEG_INF)
    vals = jnp.where(inside[None, None], _toeplitz(diag, GRID_W, GRID_W), NEG_INF)
    vals = jnp.swapaxes(vals, 2, 3) * LOG2E
    masked = jnp.full((N_HEADS, GRID_W, GRID_W), NEG_INF, F32)
    tabs = []
    for lo, step, base in ((0, 1, NA_ROWS // 2 - 1), (0, 0, NA_ROWS - 1), (NA_ROWS // 2, 0, -1)):
        rows = []
        for w in range(WIN_ROWS_C):
            per_a = [vals[:, w - a + base] if lo <= w - a * step < lo + NA_ROWS else masked
                     for a in range(ROWS_C)]
            rows.append(jnp.concatenate(per_a, axis=-1))
        tabs.append(jnp.concatenate(rows, axis=1))
    return jnp.stack(tabs)


def _layer_weights(l, norm_mix, w_in, b_q_gain, b_k_gain, d_q_gain, d_w_uq, d_kv_gain, d_w_ukv,
                   out_gain, w_out, norm_ffn, w_gate, w_up, w_down):
    d_in = w_in.shape[-1]
    kr_lo = d_in - D_ROPE
    w_in_p = jnp.zeros((D_MODEL, D_IN_PAD), F32)
    w_in_p = w_in_p.at[:, :kr_lo].set(w_in[l][:, :kr_lo])
    w_in_p = w_in_p.at[:, kr_lo + D_NOPE:kr_lo + D_NOPE + D_ROPE].set(w_in[l][:, kr_lo:])
    dqk = D_NOPE + D_ROPE
    w_uq = jnp.zeros((D_Q_LORA, N_HEADS, LANES), F32).at[:, :, :dqk].set(
        d_w_uq[l].reshape(D_Q_LORA, N_HEADS, dqk)).reshape(D_Q_LORA, N_HEADS * LANES)
    ukv = d_w_ukv[l].reshape(D_KV_LORA, N_HEADS, D_NOPE + D_V)
    w_uk = jnp.zeros((D_KV_LORA, N_HEADS, LANES), F32).at[:, :, :D_NOPE].set(
        ukv[:, :, :D_NOPE]).reshape(D_KV_LORA, N_HEADS * LANES)
    w_uv = ukv[:, :, D_NOPE:].reshape(D_KV_LORA, N_HEADS * D_V)
    return {
        "norm_mix": norm_mix[l][None, :], "w_in": w_in_p.astype(BF16),
        "b_q_gain": jnp.tile(b_q_gain[l], 2)[None, :], "b_k_gain": jnp.tile(b_k_gain[l], 2)[None, :],
        "d_q_gain": d_q_gain[l][None, :], "w_uq": w_uq.astype(BF16),
        "d_kv_gain": d_kv_gain[l][None, :], "w_uk": w_uk.astype(BF16), "w_uv": w_uv.astype(BF16),
        "out_gain": out_gain[l][None, :], "w_out": w_out[l].astype(BF16), "norm_ffn": norm_ffn[l][None, :],
        "w_gate": w_gate[l].astype(BF16), "w_up": w_up[l].astype(BF16), "w_down": w_down[l].astype(BF16),
    }


def _trunk(x, layers, tabs, na_tabs, final_norm):
    nb = x.shape[0]
    x2d = x.reshape(nb * SEQ, D_MODEL)
    for l, lw in enumerate(layers):
        a1, a4, a16, b_qt, b_k, b_vt, c_qt, c_k, c_vt, d_qt, d_k, d_vt = _proj(x2d, lw, tabs)
        outs_a = tuple(_attn_a(a_dil, tabs["band"][p], nb, dil)
                       for p, (a_dil, (_, dil)) in enumerate(zip((a1, a4, a16), DILATED_PATTERNS)))
        o_b = _attn_b(b_qt, b_k, b_vt, nb)
        o_c = _attn_c(c_qt, c_k, c_vt, na_tabs[l], nb)
        o_d = _attn_d(d_qt, d_k, d_vt, nb)
        x2d = _out_ffn(x2d, outs_a + (o_b, o_c, o_d), lw, tabs["expand"], final_norm, final=(l == DEPTH - 1))
    return x2d.reshape(nb, SEQ, D_MODEL)


def kernel(x_prompt, x_sample, t5_bias, norm_mix, w_in, b_q_gain, b_k_gain, c_rpb, d_q_gain, d_w_uq,
           d_kv_gain, d_w_ukv, out_gain, w_out, norm_ffn, w_gate, w_up, w_down, final_norm):
    tabs = _tables(t5_bias)
    na_tabs = [_na_table(c_rpb[l]) for l in range(DEPTH)]
    layers = [_layer_weights(l, norm_mix, w_in, b_q_gain, b_k_gain, d_q_gain, d_w_uq, d_kv_gain, d_w_ukv,
                             out_gain, w_out, norm_ffn, w_gate, w_up, w_down) for l in range(DEPTH)]
    fin = final_norm[None, :]
    return (_trunk(x_prompt, layers, tabs, na_tabs, fin), _trunk(x_sample, layers, tabs, na_tabs, fin))
```

```python
import functools
import math

import jax
import jax.numpy as jnp
import numpy as np
from jax import lax
from jax.experimental import pallas as pl
from jax.experimental.pallas import tpu as pltpu

F32 = jnp.float32
BF16 = jnp.bfloat16

D_MODEL = 1024
SEQ = 4096
DEPTH = 2
HEAD_DIM = 64
N_HEADS = 4
GROUP_WIDTH = N_HEADS * HEAD_DIM
DILATED_PATTERNS = ((128, 1), (512, 4), (2048, 16))
BAND_HALF = 64
DIL_MID, DIL_FAR = DILATED_PATTERNS[1][1], DILATED_PATTERNS[2][1]
T5_BUCKETS = 32
T5_MAX_DIST = 1024
GRID_W = 64
GRID_ROWS = SEQ // GRID_W
NA_ROWS = 8
NA_COLS = 16
D_Q_LORA = 256
D_KV_LORA = 128
D_NOPE = 64
D_ROPE = 32
D_V = 64
ROPE_THETA = 10000.0
D_FF = 2816
RMS_EPS = 1e-6
NEG_INF = -1e30

LANES = 128
D_IN_PAD = 2560
COL_A, COL_B, COL_C, COL_D = 0, 768, 1280, 2048
A_WIDTH = 1024
ATTN_SCALE = HEAD_DIM ** -0.5
MLA_SCALE = (D_NOPE + D_ROPE) ** -0.5
LOG2E = math.log2(math.e)

TM_PROJ = 512
PROJ_SPLIT = 2
QBLK_T = TM_PROJ // PROJ_SPLIT
QBLKS_PER_STEP = 2
QBLK_A = 128
UNROLL_A = 8
TK_DENSE = 512
SUB_DENSE = 256
ONES_ROWS = 16
ROWS_C = 4
WIN_ROWS_C = 12
TM_FFN = 512
FF_BOUNDS = (0, 768, 1536, 2304, D_FF)
VMEM_LIMIT = 56 * 1024 * 1024


def _cparams(n_axes):
    return pltpu.CompilerParams(dimension_semantics=("arbitrary",) * n_axes,
                                vmem_limit_bytes=VMEM_LIMIT)


def _rms(x, g):
    return x * lax.rsqrt(jnp.mean(x * x, axis=-1, keepdims=True) + RMS_EPS) * g


def _dot(a, b):
    return jnp.dot(a, b, preferred_element_type=F32)


def _dot_nt(a, b):
    return lax.dot_general(a, b, (((1,), (1,)), ((), ())), preferred_element_type=F32)


def _rope(y, tab):
    return y * tab[0] + pltpu.roll(y, LANES - 16, 1) * tab[1] + pltpu.roll(y, 16, 1) * tab[2]


def _proj_kernel(x_ref, gmix_ref, w_ref, bqg_ref, bkg_ref, seg_ref, tabb_ref, tabd_ref,
                 dqg_ref, wuq_ref, dkvg_ref, wuk_ref, wuv_ref,
                 a_ref, a4_ref, a16_ref, bqt_ref, bk_ref, bvt_ref, cqt_ref, ck_ref, cvt_ref, dqt_ref, dk_ref,
                 dvt_ref, a_scr, a4_scr):
    half = TM_PROJ // PROJ_SPLIT
    low_half = lax.broadcasted_iota(jnp.int32, (half, LANES), 1) < HEAD_DIM
    zeros_half = jnp.zeros((HEAD_DIM, half), BF16)
    seg = seg_ref[...]

    def main(rows):
        h = _rms(x_ref[rows, :], gmix_ref[...]).astype(BF16)
        return [_dot(h, w_ref[:, lo:hi]) for lo, hi in ((COL_A, COL_B), (COL_B, COL_C), (COL_C, COL_D),
                                                        (COL_D, D_IN_PAD))]

    def finish(jh, rows, pa, pb, pc, pd):
        slabs = []
        for hd in range(N_HEADS):
            pair = pa[:, (hd // 2) * LANES:(hd // 2 + 1) * LANES] * (ATTN_SCALE * LOG2E)
            own = low_half if hd % 2 == 0 else jnp.logical_not(low_half)
            slabs.append(jnp.where(own, pair, 0.0))
        slabs += [pa[:, GROUP_WIDTH + j * LANES:GROUP_WIDTH + (j + 1) * LANES] for j in range(4)]
        for s, val in enumerate(slabs):
            a_scr[s, rows, :] = val
            a_ref[rows, s * LANES:(s + 1) * LANES] = val.astype(BF16)

        cqt = (pc[:, :GROUP_WIDTH] * (ATTN_SCALE * LOG2E)).T.astype(BF16)
        for hd in range(N_HEADS):
            own = hd * LANES + (hd % 2) * HEAD_DIM
            other = hd * LANES + (1 - hd % 2) * HEAD_DIM
            cqt_ref[jh, own:own + HEAD_DIM, :] = cqt[hd * HEAD_DIM:(hd + 1) * HEAD_DIM]
            cqt_ref[jh, other:other + HEAD_DIM, :] = zeros_half
        ck_ref[rows, :] = pc[:, GROUP_WIDTH:2 * GROUP_WIDTH].astype(BF16)
        cvt_ref[:, rows] = pc[:, 2 * GROUP_WIDTH:].T.astype(BF16)

        tabb = tabb_ref[:, rows, :]

        def head_norm_rope(chunk, gain):
            sq = chunk * chunk
            hi = sq.astype(BF16)
            lo = (sq - hi.astype(F32)).astype(BF16)
            ms = _dot(hi, seg) + _dot(lo, seg)
            return _rope(chunk * lax.rsqrt(ms + RMS_EPS) * gain, tabb)

        for c in range(2):
            q = head_norm_rope(pb[:, c * LANES:(c + 1) * LANES], bqg_ref[...])
            qt = (q * (ATTN_SCALE * LOG2E)).T.astype(BF16)
            for j in range(2):
                r0 = (2 * c + j) * LANES
                bqt_ref[jh, r0:r0 + HEAD_DIM, :] = qt[j * HEAD_DIM:(j + 1) * HEAD_DIM]
                bqt_ref[jh, r0 + HEAD_DIM:r0 + LANES, :] = zeros_half
        kb = head_norm_rope(pb[:, 2 * LANES:3 * LANES], bkg_ref[...])
        bk_ref[rows, :LANES] = kb.astype(BF16)
        bk_ref[rows, LANES:] = pltpu.roll(kb, HEAD_DIM, 1).astype(BF16)
        bvt_ref[:, rows] = pb[:, 3 * LANES:].T.astype(BF16)

        tabd = tabd_ref[:, rows, :]
        dq = _rms(pd[:, :D_Q_LORA], dqg_ref[...]).astype(BF16)
        qd = _dot(dq, wuq_ref[...])
        dkv = _rms(pd[:, D_Q_LORA:D_Q_LORA + D_KV_LORA], dkvg_ref[...]).astype(BF16)
        kn = _dot(dkv, wuk_ref[...])
        dvt_ref[:, rows] = _dot(dkv, wuv_ref[...]).T.astype(BF16)
        kr = _rope(pd[:, D_Q_LORA + D_KV_LORA:], tabd)
        for hd in range(N_HEADS):
            sl = slice(hd * LANES, (hd + 1) * LANES)
            dqt_ref[jh, sl, :] = (_rope(qd[:, sl], tabd) * (MLA_SCALE * LOG2E)).T.astype(BF16)
            dk_ref[rows, sl] = (kn[:, sl] + kr).astype(BF16)

    row_sets = [slice(j * half, (j + 1) * half) for j in range(PROJ_SPLIT)]
    mains = [main(rows) for rows in row_sets]
    for jh, (rows, (pa, pb, pc, pd)) in enumerate(zip(row_sets, mains)):
        finish(jh, rows, pa, pb, pc, pd)

    n_slabs = A_WIDTH // LANES
    for r in range(DIL_MID):
        for s in range(n_slabs):
            val = a_scr[s, pl.ds(r, TM_PROJ // DIL_MID, stride=DIL_MID), :]
            a4_scr[r * n_slabs + s] = val
            a4_ref[:, (r * n_slabs + s) * LANES:(r * n_slabs + s + 1) * LANES] = val.astype(BF16)
    step = DIL_FAR // DIL_MID
    for j in range(step):
        for s in range(DIL_MID * n_slabs):
            c0 = (j * DIL_MID * n_slabs + s) * LANES
            a16_ref[:, c0:c0 + LANES] = a4_scr[s, pl.ds(j, TM_PROJ // DIL_FAR, stride=step), :].astype(BF16)


def _proj(x2d, lw, tabs):
    t = x2d.shape[0]
    blocks_per_seq = SEQ // TM_PROJ
    row = lambda i: (i, 0)
    const2 = lambda i: (0, 0)
    pos3 = lambda i: (0, i % blocks_per_seq, 0)

    def full(a):
        return pl.BlockSpec(a.shape, const2)

    col = lambda i: (0, i)
    outs = ((512, "qblk"), (256, "row"), (128, "col"), (512, "qblk"), (256, "row"), (256, "col"),
            (512, "qblk"), (512, "row"), (256, "col"))
    out_spec = {"row": lambda w: pl.BlockSpec((TM_PROJ, w), row),
                "col": lambda w: pl.BlockSpec((w, TM_PROJ), col),
                "qblk": lambda w: pl.BlockSpec((PROJ_SPLIT, w, QBLK_T), lambda i: (i, 0, 0))}
    out_shape = {"row": lambda w: (t, w), "col": lambda w: (w, t), "qblk": lambda w: (t // QBLK_T, w, QBLK_T)}
    a_specs = [pl.BlockSpec((TM_PROJ // d, d * A_WIDTH), row) for _, d in DILATED_PATTERNS]
    a_shapes = [jax.ShapeDtypeStruct((t // d, d * A_WIDTH), BF16) for _, d in DILATED_PATTERNS]
    return pl.pallas_call(
        _proj_kernel,
        grid=(t // TM_PROJ,),
        scratch_shapes=[pltpu.VMEM((A_WIDTH // LANES, TM_PROJ, LANES), F32),
                        pltpu.VMEM((DIL_MID * A_WIDTH // LANES, TM_PROJ // DIL_MID, LANES), F32)],
        in_specs=[pl.BlockSpec((TM_PROJ, D_MODEL), row), full(lw["norm_mix"]), full(lw["w_in"]),
                  full(lw["b_q_gain"]), full(lw["b_k_gain"]), full(tabs["seg"]),
                  pl.BlockSpec((3, TM_PROJ, LANES), pos3), pl.BlockSpec((3, TM_PROJ, LANES), pos3),
                  full(lw["d_q_gain"]), full(lw["w_uq"]), full(lw["d_kv_gain"]), full(lw["w_uk"]),
                  full(lw["w_uv"])],
        out_specs=a_specs + [out_spec[kind](w) for w, kind in outs],
        out_shape=a_shapes + [jax.ShapeDtypeStruct(out_shape[kind](w), BF16) for w, kind in outs],
        compiler_params=_cparams(1),
        name="proj",
    )(x2d, lw["norm_mix"], lw["w_in"], lw["b_q_gain"], lw["b_k_gain"], tabs["seg"], tabs["rope_b"],
      tabs["rope_d"], lw["d_q_gain"], lw["w_uq"], lw["d_kv_gain"], lw["w_uk"], lw["w_uv"])


def _attn_a_kernel(q_ref, k_ref, v_ref, tab_ref, o_ref, st_ref, *, n):
    nblk = n // QBLK_A
    lane = lax.broadcasted_iota(jnp.int32, (QBLK_A, LANES), 1)
    ones = jnp.ones((2 * QBLK_A, LANES), BF16)

    def body(i, carry):
        rows = pl.ds(pl.multiple_of(i * QBLK_A, QBLK_A), QBLK_A)
        win = pl.ds(pl.multiple_of(jnp.clip(i * QBLK_A - BAND_HALF, 0, n - 2 * QBLK_A), BAND_HALF), 2 * QBLK_A)
        var = jnp.where(i == 0, 1, jnp.where(i == nblk - 1, 2, 0))

        def scores(hd):
            pair = slice((hd // 2) * LANES, (hd // 2 + 1) * LANES)
            return _dot_nt(q_ref[rows, hd * LANES:(hd + 1) * LANES], k_ref[win, pair]) + tab_ref[var, hd]

        st = jnp.zeros((QBLK_A, LANES), F32)
        s_next = scores(0)
        for hd in range(N_HEADS):
            s = s_next
            if hd + 1 < N_HEADS:
                s_next = scores(hd + 1)
            pair = slice((hd // 2) * LANES, (hd // 2 + 1) * LANES)
            m = jnp.max(s, axis=-1, keepdims=True)
            e = jnp.exp2(s - m).astype(BF16)
            pv = _dot(e, jnp.concatenate([v_ref[win, pair], ones], axis=1))
            l = pv[:, LANES:]
            o_h = pv[:, :LANES] / l
            st = jnp.where(lane == hd, m + jnp.log2(l), st)
            if hd % 2 == 0:
                o_even = o_h
            else:
                o_ref[rows, pair] = jnp.where(lane < HEAD_DIM, o_even, o_h)
        st_ref[rows, :] = st
        return carry

    lax.fori_loop(0, nblk, body, 0, unroll=min(UNROLL_A, nblk))


def _attn_a(a_dil, tab, nb, dil):
    n = SEQ // dil
    view = a_dil.reshape(nb, n, dil * A_WIDTH)
    o, st = pl.pallas_call(
        functools.partial(_attn_a_kernel, n=n),
        grid=(nb, dil),
        in_specs=[pl.BlockSpec((None, n, 2 * GROUP_WIDTH), lambda b, r: (b, 0, 2 * r)),
                  pl.BlockSpec((None, n, GROUP_WIDTH), lambda b, r: (b, 0, 4 * r + 2)),
                  pl.BlockSpec((None, n, GROUP_WIDTH), lambda b, r: (b, 0, 4 * r + 3)),
                  pl.BlockSpec(tab.shape, lambda b, r: (0, 0, 0, 0))],
        out_specs=[pl.BlockSpec((None, n, GROUP_WIDTH), lambda b, r: (b, 0, r)),
                   pl.BlockSpec((None, n, LANES), lambda b, r: (b, 0, r))],
        out_shape=[jax.ShapeDtypeStruct((nb, n, dil * GROUP_WIDTH), F32),
                   jax.ShapeDtypeStruct((nb, n, dil * LANES), F32)],
        compiler_params=_cparams(2),
        name=f"attn_a_d{dil}",
    )(view, view, view, tab)
    return o.reshape(nb * n, dil * GROUP_WIDTH), st.reshape(nb * n, dil * LANES)


def _flash(problems):
    n_chunks = SEQ // TK_DENSE
    ones = jnp.ones((ONES_ROWS, TK_DENSE), BF16)
    n_sub = TK_DENSE // SUB_DENSE
    state = [None] * len(problems)

    def scores(c):
        return [[_dot(k_at(c)[j * SUB_DENSE:(j + 1) * SUB_DENSE], q_t) for j in range(n_sub)]
                for q_t, k_at, _ in problems]

    s_cur = scores(0)
    for c in range(n_chunks):
        if c + 1 < n_chunks:
            s_next = scores(c + 1)
        for i, (subs, (_, _, v_t_at)) in enumerate(zip(s_cur, problems)):
            col_max = functools.reduce(jnp.maximum, [jnp.max(s, axis=0, keepdims=True) for s in subs])
            m_new = col_max if c == 0 else jnp.maximum(state[i][0], col_max)
            v_ext = jnp.concatenate([v_t_at(c), ones], axis=0)
            acc = None if c == 0 else jnp.exp2(state[i][0] - m_new) * state[i][1]
            for j, s in enumerate(subs):
                part = _dot(v_ext[:, j * SUB_DENSE:(j + 1) * SUB_DENSE], jnp.exp2(s - m_new).astype(BF16))
                acc = part if acc is None else acc + part
            state[i] = (m_new, acc)
        s_cur = s_next
    outs = []
    for (_, _, v_t_at), (_, acc) in zip(problems, state):
        dv = acc.shape[0] - ONES_ROWS
        outs.append(acc[:dv] / acc[dv:dv + 1])
    return outs


def _chunk(c):
    return pl.ds(c * TK_DENSE, TK_DENSE)


def _store_blocks(o_ref, block_out):
    def body(jb, carry):
        o_ref[pl.ds(pl.multiple_of(jb * QBLK_T, QBLK_T), QBLK_T), :] = block_out(jb)
        return carry

    lax.fori_loop(0, QBLKS_PER_STEP, body, 0)


def _attn_b_kernel(qt_ref, k_ref, vt_ref, o_ref):
    def block_out(jb):
        problems = []
        for hd in range(N_HEADS):
            g = hd // 2
            problems.append((qt_ref[jb, hd * LANES:(hd + 1) * LANES, :],
                             lambda c, g=g: k_ref[_chunk(c), g * LANES:(g + 1) * LANES],
                             lambda c, g=g: vt_ref[g * HEAD_DIM:(g + 1) * HEAD_DIM, _chunk(c)]))
        return jnp.concatenate(_flash(problems), axis=0).T

    _store_blocks(o_ref, block_out)


def _attn_b(b_qt, b_k, b_vt, nb):
    tq = QBLK_T * QBLKS_PER_STEP
    nq = SEQ // tq
    o = pl.pallas_call(
        _attn_b_kernel,
        grid=(nb, nq),
        in_specs=[pl.BlockSpec((QBLKS_PER_STEP, N_HEADS * LANES, QBLK_T), lambda b, i: (b * nq + i, 0, 0)),
                  pl.BlockSpec((None, SEQ, 2 * LANES), lambda b, i: (b, 0, 0)),
                  pl.BlockSpec((2 * HEAD_DIM, SEQ), lambda b, i: (0, b))],
        out_specs=pl.BlockSpec((None, tq, GROUP_WIDTH), lambda b, i: (b, i, 0)),
        out_shape=jax.ShapeDtypeStruct((nb, SEQ, GROUP_WIDTH), F32),
        compiler_params=_cparams(2),
        name="attn_b",
    )(b_qt, b_k.reshape(nb, SEQ, 2 * LANES), b_vt)
    return o.reshape(nb * SEQ, GROUP_WIDTH)


def _attn_d_kernel(qt_ref, k_ref, vt_ref, o_ref):
    def block_out(jb):
        problems = []
        for j in range(2):
            problems.append((qt_ref[jb, j * LANES:(j + 1) * LANES, :],
                             lambda c, j=j: k_ref[_chunk(c), j * LANES:(j + 1) * LANES],
                             lambda c, j=j: vt_ref[j * D_V:(j + 1) * D_V, _chunk(c)]))
        return jnp.concatenate(_flash(problems), axis=0).T

    _store_blocks(o_ref, block_out)


def _attn_d(d_qt, d_k, d_vt, nb):
    tq = QBLK_T * QBLKS_PER_STEP
    nq = SEQ // tq
    o = pl.pallas_call(
        _attn_d_kernel,
        grid=(nb, 2, nq),
        in_specs=[pl.BlockSpec((QBLKS_PER_STEP, 2 * LANES, QBLK_T), lambda b, p, i: (b * nq + i, p, 0)),
                  pl.BlockSpec((None, SEQ, 2 * LANES), lambda b, p, i: (b, 0, p)),
                  pl.BlockSpec((2 * D_V, SEQ), lambda b, p, i: (p, b))],
        out_specs=pl.BlockSpec((None, tq, LANES), lambda b, p, i: (b, i, p)),
        out_shape=jax.ShapeDtypeStruct((nb, SEQ, GROUP_WIDTH), F32),
        compiler_params=_cparams(3),
        name="attn_d",
    )(d_qt, d_k.reshape(nb, SEQ, N_HEADS * LANES), d_vt)
    return o.reshape(nb * SEQ, GROUP_WIDTH)


def _attn_c_kernel(qt_ref, k_ref, vt_ref, tab_ref, o_ref):
    n_groups = GRID_ROWS // ROWS_C
    ones = jnp.ones((ONES_ROWS, WIN_ROWS_C * GRID_W), BF16)

    def block_out(jb):
        i = pl.program_id(1) * QBLKS_PER_STEP + jb
        var = jnp.where(i == 0, 1, jnp.where(i == n_groups - 1, 2, 0))
        k0 = pl.multiple_of(jnp.clip(i * ROWS_C - NA_ROWS // 2, 0, GRID_ROWS - WIN_ROWS_C) * GRID_W,
                            ROWS_C * GRID_W)
        keys = pl.ds(k0, WIN_ROWS_C * GRID_W)

        def scores(hd):
            pair = slice((hd // 2) * LANES, (hd // 2 + 1) * LANES)
            return _dot(k_ref[keys, pair], qt_ref[jb, hd * LANES:(hd + 1) * LANES, :]) + tab_ref[var, hd]

        outs = []
        s_next = scores(0)
        for hd in range(N_HEADS):
            s = s_next
            if hd + 1 < N_HEADS:
                s_next = scores(hd + 1)
            p = jnp.exp2(s - jnp.max(s, axis=0, keepdims=True)).astype(BF16)
            v_ext = jnp.concatenate([vt_ref[hd * HEAD_DIM:(hd + 1) * HEAD_DIM, keys], ones], axis=0)
            acc = _dot(v_ext, p)
            outs.append(acc[:HEAD_DIM] / acc[HEAD_DIM:HEAD_DIM + 1])
        return jnp.concatenate(outs, axis=0).T

    _store_blocks(o_ref, block_out)


def _attn_c(c_qt, c_k, c_vt, tab, nb):
    assert QBLK_T == ROWS_C * GRID_W
    tq = QBLK_T * QBLKS_PER_STEP
    nq = SEQ // tq
    o = pl.pallas_call(
        _attn_c_kernel,
        grid=(nb, nq),
        in_specs=[pl.BlockSpec((QBLKS_PER_STEP, N_HEADS * LANES, QBLK_T), lambda b, i: (b * nq + i, 0, 0)),
                  pl.BlockSpec((None, SEQ, GROUP_WIDTH), lambda b, i: (b, 0, 0)),
                  pl.BlockSpec((GROUP_WIDTH, SEQ), lambda b, i: (0, b)),
                  pl.BlockSpec(tab.shape, lambda b, i: (0, 0, 0, 0), pipeline_mode=pl.Buffered(1))],
        out_specs=pl.BlockSpec((None, tq, GROUP_WIDTH), lambda b, i: (b, i, 0)),
        out_shape=jax.ShapeDtypeStruct((nb, SEQ, GROUP_WIDTH), F32),
        compiler_params=_cparams(2),
        name="attn_c",
    )(c_qt, c_k.reshape(nb, SEQ, GROUP_WIDTH), c_vt, tab)
    return o.reshape(nb * SEQ, GROUP_WIDTH)


def _out_ffn_kernel(x_ref, oa1_ref, sa1_ref, oa2_ref, sa2_ref, oa3_ref, sa3_ref, ob_ref, oc_ref, od_ref,
                    expand_ref, og_ref, wout_ref, nf_ref, wg_ref, wu_ref, wd_ref, fin_ref, y_ref,
                    o_scr, s_scr, *, final):
    for j, (dil, o_ref, s_ref) in enumerate(((DIL_MID, oa2_ref, sa2_ref), (DIL_FAR, oa3_ref, sa3_ref))):
        for r in range(dil):
            dst = pl.ds(r, TM_FFN // dil, stride=dil)
            for s in range(GROUP_WIDTH // LANES):
                c0 = r * GROUP_WIDTH + s * LANES
                o_scr[2 * j + s, dst, :] = o_ref[:, c0:c0 + LANES]
            s_scr[j, dst, :] = s_ref[:, r * LANES:(r + 1) * LANES]
    oa = [oa1_ref[...]] + [jnp.concatenate([o_scr[2 * j], o_scr[2 * j + 1]], axis=1) for j in range(2)]

    lses = [sa1_ref[...], s_scr[0], s_scr[1]]
    mx = jnp.maximum(jnp.maximum(lses[0], lses[1]), lses[2])
    es = [jnp.exp2(s - mx) for s in lses]
    den = es[0] + es[1] + es[2]
    expand = expand_ref[...]

    def per_head(w):
        hi = w.astype(BF16)
        lo = (w - hi.astype(F32)).astype(BF16)
        return _dot(hi, expand) + _dot(lo, expand)

    o_a = per_head(es[0] / den) * oa[0] + per_head(es[1] / den) * oa[1] + per_head(es[2] / den) * oa[2]

    x = x_ref[...]
    groups = (o_a, ob_ref[...], oc_ref[...], od_ref[...])
    normed = [_rms(o, og_ref[:, g * GROUP_WIDTH:(g + 1) * GROUP_WIDTH]).astype(BF16) for g, o in enumerate(groups)]
    x1 = x + _dot(jnp.concatenate(normed, axis=1), wout_ref[...])

    h = _rms(x1, nf_ref[...]).astype(BF16)

    def gate_up(c):
        cols = slice(FF_BOUNDS[c], FF_BOUNDS[c + 1])
        gate = _dot(h, wg_ref[:, cols])
        up = _dot(h, wu_ref[:, cols])
        return (gate * (1.0 / (1.0 + jnp.exp(-gate))) * up).astype(BF16)

    n_ff = len(FF_BOUNDS) - 1
    act = gate_up(0)
    ff = None
    for c in range(n_ff):
        act_next = gate_up(c + 1) if c + 1 < n_ff else None
        part = _dot(act, wd_ref[FF_BOUNDS[c]:FF_BOUNDS[c + 1], :])
        ff = part if ff is None else ff + part
        act = act_next
    x2 = x1 + ff
    if final:
        x2 = _rms(x2, fin_ref[...])
    y_ref[...] = x2


def _out_ffn(x2d, attn_outs, lw, expand, final_norm, final):
    t = x2d.shape[0]
    row = lambda i: (i, 0)

    def resident(a):
        return pl.BlockSpec(a.shape, lambda i: (0, 0), pipeline_mode=pl.Buffered(1))

    (oa1, sa1), (oa2, sa2), (oa3, sa3), ob, oc, od = attn_outs
    wide = pl.BlockSpec((TM_FFN, GROUP_WIDTH), row)
    a_specs = [spec for _, d in DILATED_PATTERNS
               for spec in (pl.BlockSpec((TM_FFN // d, d * GROUP_WIDTH), row),
                            pl.BlockSpec((TM_FFN // d, d * LANES), row))]
    weights = (expand, lw["out_gain"], lw["w_out"], lw["norm_ffn"], lw["w_gate"], lw["w_up"], lw["w_down"],
               final_norm)
    return pl.pallas_call(
        functools.partial(_out_ffn_kernel, final=final),
        grid=(t // TM_FFN,),
        in_specs=[pl.BlockSpec((TM_FFN, D_MODEL), row)] + a_specs + [wide, wide, wide]
                 + [resident(w) for w in weights],
        out_specs=pl.BlockSpec((TM_FFN, D_MODEL), row),
        out_shape=jax.ShapeDtypeStruct((t, D_MODEL), F32),
        scratch_shapes=[pltpu.VMEM((2 * GROUP_WIDTH // LANES, TM_FFN, LANES), F32),
                        pltpu.VMEM((2, TM_FFN, LANES), F32)],
        compiler_params=_cparams(1),
        name="out_ffn",
    )(x2d, oa1, sa1, oa2, sa2, oa3, sa3, ob, oc, od, *weights)


def _rope_angles(pos, dim):
    inv = 1.0 / (ROPE_THETA ** (jnp.arange(0, dim, 2, dtype=F32) / dim))
    return pos.astype(F32)[:, None] * inv[None, :]


def _t5_bucket(rel):
    nb = T5_BUCKETS // 2
    max_exact = nb // 2
    n = jnp.abs(rel)
    n_f = jnp.maximum(n, max_exact).astype(F32)
    large = max_exact + (jnp.log(n_f / max_exact) / math.log(T5_MAX_DIST / max_exact)
                         * (nb - max_exact)).astype(jnp.int32)
    large = jnp.minimum(large, nb - 1)
    return jnp.where(rel > 0, nb, 0) + jnp.where(n < max_exact, n, large)


def _tables(t5_bias):
    t = jnp.arange(SEQ, dtype=jnp.int32)
    z16 = jnp.zeros((SEQ, 16), F32)

    def trio(parts):
        return jnp.stack([jnp.concatenate(p, axis=1) for p in parts])

    ang_r = _rope_angles(t // GRID_W, HEAD_DIM // 2)
    ang_c = _rope_angles(t % GRID_W, HEAD_DIM // 2)
    cr, sr, cc, sc = jnp.cos(ang_r), jnp.sin(ang_r), jnp.cos(ang_c), jnp.sin(ang_c)
    rope_b = trio(([cr, cr, cc, cc] * 2, [-sr, z16, -sc, z16] * 2, [z16, sr, z16, sc] * 2))

    ang_t = _rope_angles(t, D_ROPE)
    ct, st = jnp.cos(ang_t), jnp.sin(ang_t)
    ones64, z64, z32 = jnp.ones((SEQ, 64), F32), jnp.zeros((SEQ, 64), F32), jnp.zeros((SEQ, 32), F32)
    rope_d = trio(([ones64, ct, ct, z32], [z64, -st, z16, z32], [z64, z16, st, z32]))

    seg = jnp.kron(jnp.eye(2, dtype=F32), jnp.full((HEAD_DIM, HEAD_DIM), 1.0 / HEAD_DIM, F32)).astype(BF16)
    expand = jnp.zeros((LANES, GROUP_WIDTH), F32).at[:N_HEADS].set(
        jnp.kron(jnp.eye(N_HEADS, dtype=F32), jnp.ones((1, HEAD_DIM), F32))).astype(BF16)

    period = 4 * QBLK_A
    u = np.arange(period)
    rel = np.where(u < period - QBLK_A, u, u - period)
    band = []
    for _, dil in DILATED_PATTERNS:
        off = jnp.arange(-BAND_HALF, BAND_HALF + 1, dtype=jnp.int32) * dil
        bias = t5_bias[_t5_bucket(off)].T.astype(F32)
        variants = []
        for shift in (-BAND_HALF, 0, -2 * BAND_HALF):
            delta = rel + shift
            inside = np.abs(delta) <= BAND_HALF
            diag = jnp.where(inside[None], bias[:, np.clip(delta + BAND_HALF, 0, 2 * BAND_HALF)] * LOG2E,
                             NEG_INF)
            variants.append(_toeplitz(diag, QBLK_A, 2 * QBLK_A))
        band.append(jnp.stack(variants))
    return {"rope_b": rope_b, "rope_d": rope_d, "seg": seg, "expand": expand, "band": band}


def _toeplitz(w, n_rows, n_cols):
    period = w.shape[-1]
    tiled = jnp.tile(w, (1,) * (w.ndim - 1) + (n_rows,))
    return tiled[..., :n_rows * (period - 1)].reshape(w.shape[:-1] + (n_rows, period - 1))[..., :n_cols]


def _na_table(rpb):
    cols = np.arange(GRID_W)
    cs = np.clip(cols - NA_COLS // 2, 0, GRID_W - NA_COLS)
    inside = (cols[None, :] >= cs[:, None]) & (cols[None, :] < cs[:, None] + NA_COLS)
    period = 2 * GRID_W
    u = np.arange(period)
    dc = np.where(u < GRID_W, u, u - period) + NA_COLS - 1
    diag = jnp.where(((dc >= 0) & (dc <= 2 * NA_COLS - 2))[None, None],
                     rpb.astype(F32)[:, :, np.clip(dc, 0, 2 * NA_COLS - 2)], NEG_INF)
    vals = jnp.where(inside[None, None], _toeplitz(diag, GRID_W, GRID_W), NEG_INF)
    vals = jnp.swapaxes(vals, 2, 3) * LOG2E
    masked = jnp.full((N_HEADS, GRID_W, GRID_W), NEG_INF, F32)
    tabs = []
    for lo, step, base in ((0, 1, NA_ROWS // 2 - 1), (0, 0, NA_ROWS - 1), (NA_ROWS // 2, 0, -1)):
        rows = []
        for w in range(WIN_ROWS_C):
            per_a = [vals[:, w - a + base] if lo <= w - a * step < lo + NA_ROWS else masked
                     for a in range(ROWS_C)]
            rows.append(jnp.concatenate(per_a, axis=-1))
        tabs.append(jnp.concatenate(rows, axis=1))
    return jnp.stack(tabs)


def _layer_weights(l, norm_mix, w_in, b_q_gain, b_k_gain, d_q_gain, d_w_uq, d_kv_gain, d_w_ukv,
                   out_gain, w_out, norm_ffn, w_gate, w_up, w_down):
    d_in = w_in.shape[-1]
    kr_lo = d_in - D_ROPE
    w_in_p = jnp.zeros((D_MODEL, D_IN_PAD), F32)
    w_in_p = w_in_p.at[:, :kr_lo].set(w_in[l][:, :kr_lo])
    w_in_p = w_in_p.at[:, kr_lo + D_NOPE:kr_lo + D_NOPE + D_ROPE].set(w_in[l][:, kr_lo:])
    dqk = D_NOPE + D_ROPE
    w_uq = jnp.zeros((D_Q_LORA, N_HEADS, LANES), F32).at[:, :, :dqk].set(
        d_w_uq[l].reshape(D_Q_LORA, N_HEADS, dqk)).reshape(D_Q_LORA, N_HEADS * LANES)
    ukv = d_w_ukv[l].reshape(D_KV_LORA, N_HEADS, D_NOPE + D_V)
    w_uk = jnp.zeros((D_KV_LORA, N_HEADS, LANES), F32).at[:, :, :D_NOPE].set(
        ukv[:, :, :D_NOPE]).reshape(D_KV_LORA, N_HEADS * LANES)
    w_uv = ukv[:, :, D_NOPE:].reshape(D_KV_LORA, N_HEADS * D_V)
    return {
        "norm_mix": norm_mix[l][None, :], "w_in": w_in_p.astype(BF16),
        "b_q_gain": jnp.tile(b_q_gain[l], 2)[None, :], "b_k_gain": jnp.tile(b_k_gain[l], 2)[None, :],
        "d_q_gain": d_q_gain[l][None, :], "w_uq": w_uq.astype(BF16),
        "d_kv_gain": d_kv_gain[l][None, :], "w_uk": w_uk.astype(BF16), "w_uv": w_uv.astype(BF16),
        "out_gain": out_gain[l][None, :], "w_out": w_out[l].astype(BF16), "norm_ffn": norm_ffn[l][None, :],
        "w_gate": w_gate[l].astype(BF16), "w_up": w_up[l].astype(BF16), "w_down": w_down[l].astype(BF16),
    }


def _trunk(x, layers, tabs, na_tabs, final_norm):
    nb = x.shape[0]
    x2d = x.reshape(nb * SEQ, D_MODEL)
    for l, lw in enumerate(layers):
        a1, a4, a16, b_qt, b_k, b_vt, c_qt, c_k, c_vt, d_qt, d_k, d_vt = _proj(x2d, lw, tabs)
        outs_a = tuple(_attn_a(a_dil, tabs["band"][p], nb, dil)
                       for p, (a_dil, (_, dil)) in enumerate(zip((a1, a4, a16), DILATED_PATTERNS)))
        o_b = _attn_b(b_qt, b_k, b_vt, nb)
        o_c = _attn_c(c_qt, c_k, c_vt, na_tabs[l], nb)
        o_d = _attn_d(d_qt, d_k, d_vt, nb)
        x2d = _out_ffn(x2d, outs_a + (o_b, o_c, o_d), lw, tabs["expand"], final_norm, final=(l == DEPTH - 1))
    return x2d.reshape(nb, SEQ, D_MODEL)


def kernel(x_prompt, x_sample, t5_bias, norm_mix, w_in, b_q_gain, b_k_gain, c_rpb, d_q_gain, d_w_uq,
           d_kv_gain, d_w_ukv, out_gain, w_out, norm_ffn, w_gate, w_up, w_down, final_norm):
    tabs = _tables(t5_bias)
    na_tabs = [_na_table(c_rpb[l]) for l in range(DEPTH)]
    layers = [_layer_weights(l, norm_mix, w_in, b_q_gain, b_k_gain, d_q_gain, d_w_uq, d_kv_gain, d_w_ukv,
                             out_gain, w_out, norm_ffn, w_gate, w_up, w_down) for l in range(DEPTH)]
    fin = final_norm[None, :]
    return (_trunk(x_prompt, layers, tabs, na_tabs, fin), _trunk(x_sample, layers, tabs, na_tabs, fin))
```

```python
import functools
import math

import jax
import jax.numpy as jnp
import numpy as np
from jax import lax
from jax.experimental import pallas as pl
from jax.experimental.pallas import tpu as pltpu

F32 = jnp.float32
BF16 = jnp.bfloat16

D_MODEL = 1024
SEQ = 4096
DEPTH = 2
HEAD_DIM = 64
N_HEADS = 4
GROUP_WIDTH = N_HEADS * HEAD_DIM
DILATED_PATTERNS = ((128, 1), (512, 4), (2048, 16))
BAND_HALF = 64
DIL_MID, DIL_FAR = DILATED_PATTERNS[1][1], DILATED_PATTERNS[2][1]
T5_BUCKETS = 32
T5_MAX_DIST = 1024
GRID_W = 64
GRID_ROWS = SEQ // GRID_W
NA_ROWS = 8
NA_COLS = 16
D_Q_LORA = 256
D_KV_LORA = 128
D_NOPE = 64
D_ROPE = 32
D_V = 64
ROPE_THETA = 10000.0
D_FF = 2816
RMS_EPS = 1e-6
NEG_INF = -1e30

LANES = 128
D_IN_PAD = 2560
COL_A, COL_B, COL_C, COL_D = 0, 768, 1280, 2048
A_WIDTH = 1024
ATTN_SCALE = HEAD_DIM ** -0.5
MLA_SCALE = (D_NOPE + D_ROPE) ** -0.5
LOG2E = math.log2(math.e)

TM_PROJ = 512
PROJ_SPLIT = 2
QBLK_T = TM_PROJ // PROJ_SPLIT
QBLKS_PER_STEP = 4
QBLK_A = 128
RES_PER_STEP_A = {1: 1, 4: 2, 16: 4}
UNROLL_A = 8
TK_DENSE = 512
SUB_DENSE = 256
ONES_ROWS = 16
ROWS_C = 4
WIN_ROWS_C = 12
TM_FFN = 512
FF_BOUNDS = (0, 768, 1536, 2304, D_FF)
VMEM_LIMIT = 56 * 1024 * 1024


def _cparams(n_axes):
    return pltpu.CompilerParams(dimension_semantics=("arbitrary",) * n_axes,
                                vmem_limit_bytes=VMEM_LIMIT)


def _rms(x, g):
    return x * lax.rsqrt(jnp.mean(x * x, axis=-1, keepdims=True) + RMS_EPS) * g


def _dot(a, b):
    return jnp.dot(a, b, preferred_element_type=F32)


def _dot_nt(a, b):
    return lax.dot_general(a, b, (((1,), (1,)), ((), ())), preferred_element_type=F32)


def _rope(y, tab):
    return y * tab[0] + pltpu.roll(y, LANES - 16, 1) * tab[1] + pltpu.roll(y, 16, 1) * tab[2]


def _proj_kernel(x_ref, gmix_ref, w_ref, bqg_ref, bkg_ref, seg_ref, tabb_ref, tabd_ref,
                 dqg_ref, wuq_ref, dkvg_ref, wuk_ref, wuv_ref,
                 a_ref, a4_ref, a16_ref, bqt_ref, bk_ref, bvt_ref, cqt_ref, ck_ref, cvt_ref, dqt_ref, dk_ref,
                 dvt_ref, a_scr, a4_scr):
    half = TM_PROJ // PROJ_SPLIT
    low_half = lax.broadcasted_iota(jnp.int32, (half, LANES), 1) < HEAD_DIM
    zeros_half = jnp.zeros((HEAD_DIM, half), BF16)
    seg = seg_ref[...]

    def main(rows):
        h = _rms(x_ref[rows, :], gmix_ref[...]).astype(BF16)
        return [_dot(h, w_ref[:, lo:hi]) for lo, hi in ((COL_A, COL_B), (COL_B, COL_C), (COL_C, COL_D),
                                                        (COL_D, D_IN_PAD))]

    def finish(jh, rows, pa, pb, pc, pd):
        slabs = []
        for hd in range(N_HEADS):
            pair = pa[:, (hd // 2) * LANES:(hd // 2 + 1) * LANES] * (ATTN_SCALE * LOG2E)
            own = low_half if hd % 2 == 0 else jnp.logical_not(low_half)
            slabs.append(jnp.where(own, pair, 0.0))
        slabs += [pa[:, GROUP_WIDTH + j * LANES:GROUP_WIDTH + (j + 1) * LANES] for j in range(4)]
        for s, val in enumerate(slabs):
            a_scr[s, rows, :] = val
            a_ref[rows, s * LANES:(s + 1) * LANES] = val.astype(BF16)

        cqt = (pc[:, :GROUP_WIDTH] * (ATTN_SCALE * LOG2E)).T.astype(BF16)
        for hd in range(N_HEADS):
            own = hd * LANES + (hd % 2) * HEAD_DIM
            other = hd * LANES + (1 - hd % 2) * HEAD_DIM
            cqt_ref[jh, own:own + HEAD_DIM, :] = cqt[hd * HEAD_DIM:(hd + 1) * HEAD_DIM]
            cqt_ref[jh, other:other + HEAD_DIM, :] = zeros_half
        ck_ref[rows, :] = pc[:, GROUP_WIDTH:2 * GROUP_WIDTH].astype(BF16)
        cvt_ref[:, rows] = pc[:, 2 * GROUP_WIDTH:].T.astype(BF16)

        tabb = tabb_ref[:, rows, :]

        def head_norm_rope(chunk, gain):
            sq = chunk * chunk
            hi = sq.astype(BF16)
            lo = (sq - hi.astype(F32)).astype(BF16)
            ms = _dot(hi, seg) + _dot(lo, seg)
            return _rope(chunk * lax.rsqrt(ms + RMS_EPS) * gain, tabb)

        for c in range(2):
            q = head_norm_rope(pb[:, c * LANES:(c + 1) * LANES], bqg_ref[...])
            qt = (q * (ATTN_SCALE * LOG2E)).T.astype(BF16)
            for j in range(2):
                r0 = (2 * c + j) * LANES
                bqt_ref[jh, r0:r0 + HEAD_DIM, :] = qt[j * HEAD_DIM:(j + 1) * HEAD_DIM]
                bqt_ref[jh, r0 + HEAD_DIM:r0 + LANES, :] = zeros_half
        kb = head_norm_rope(pb[:, 2 * LANES:3 * LANES], bkg_ref[...])
        bk_ref[rows, :LANES] = kb.astype(BF16)
        bk_ref[rows, LANES:] = pltpu.roll(kb, HEAD_DIM, 1).astype(BF16)
        bvt_ref[:, rows] = pb[:, 3 * LANES:].T.astype(BF16)

        tabd = tabd_ref[:, rows, :]
        dq = _rms(pd[:, :D_Q_LORA], dqg_ref[...]).astype(BF16)
        qd = _dot(dq, wuq_ref[...])
        dkv = _rms(pd[:, D_Q_LORA:D_Q_LORA + D_KV_LORA], dkvg_ref[...]).astype(BF16)
        kn = _dot(dkv, wuk_ref[...])
        dvt_ref[:, rows] = _dot(dkv, wuv_ref[...]).T.astype(BF16)
        kr = _rope(pd[:, D_Q_LORA + D_KV_LORA:], tabd)
        for hd in range(N_HEADS):
            sl = slice(hd * LANES, (hd + 1) * LANES)
            dqt_ref[jh, sl, :] = (_rope(qd[:, sl], tabd) * (MLA_SCALE * LOG2E)).T.astype(BF16)
            dk_ref[rows, sl] = (kn[:, sl] + kr).astype(BF16)

    row_sets = [slice(j * half, (j + 1) * half) for j in range(PROJ_SPLIT)]
    mains = [main(rows) for rows in row_sets]
    for jh, (rows, (pa, pb, pc, pd)) in enumerate(zip(row_sets, mains)):
        finish(jh, rows, pa, pb, pc, pd)

    n_slabs = A_WIDTH // LANES
    for r in range(DIL_MID):
        for s in range(n_slabs):
            val = a_scr[s, pl.ds(r, TM_PROJ // DIL_MID, stride=DIL_MID), :]
            a4_scr[r * n_slabs + s] = val
            a4_ref[:, (r * n_slabs + s) * LANES:(r * n_slabs + s + 1) * LANES] = val.astype(BF16)
    step = DIL_FAR // DIL_MID
    for j in range(step):
        for s in range(DIL_MID * n_slabs):
            c0 = (j * DIL_MID * n_slabs + s) * LANES
            a16_ref[:, c0:c0 + LANES] = a4_scr[s, pl.ds(j, TM_PROJ // DIL_FAR, stride=step), :].astype(BF16)


def _proj(x2d, lw, tabs):
    t = x2d.shape[0]
    blocks_per_seq = SEQ // TM_PROJ
    row = lambda i: (i, 0)
    const2 = lambda i: (0, 0)
    pos3 = lambda i: (0, i % blocks_per_seq, 0)

    def full(a):
        return pl.BlockSpec(a.shape, const2)

    col = lambda i: (0, i)
    outs = ((512, "qblk"), (256, "row"), (128, "col"), (512, "qblk"), (256, "row"), (256, "col"),
            (512, "qblk"), (512, "row"), (256, "col"))
    out_spec = {"row": lambda w: pl.BlockSpec((TM_PROJ, w), row),
                "col": lambda w: pl.BlockSpec((w, TM_PROJ), col),
                "qblk": lambda w: pl.BlockSpec((PROJ_SPLIT, w, QBLK_T), lambda i: (i, 0, 0))}
    out_shape = {"row": lambda w: (t, w), "col": lambda w: (w, t), "qblk": lambda w: (t // QBLK_T, w, QBLK_T)}
    a_specs = [pl.BlockSpec((TM_PROJ // d, d * A_WIDTH), row) for _, d in DILATED_PATTERNS]
    a_shapes = [jax.ShapeDtypeStruct((t // d, d * A_WIDTH), BF16) for _, d in DILATED_PATTERNS]
    return pl.pallas_call(
        _proj_kernel,
        grid=(t // TM_PROJ,),
        scratch_shapes=[pltpu.VMEM((A_WIDTH // LANES, TM_PROJ, LANES), F32),
                        pltpu.VMEM((DIL_MID * A_WIDTH // LANES, TM_PROJ // DIL_MID, LANES), F32)],
        in_specs=[pl.BlockSpec((TM_PROJ, D_MODEL), row), full(lw["norm_mix"]), full(lw["w_in"]),
                  full(lw["b_q_gain"]), full(lw["b_k_gain"]), full(tabs["seg"]),
                  pl.BlockSpec((3, TM_PROJ, LANES), pos3), pl.BlockSpec((3, TM_PROJ, LANES), pos3),
                  full(lw["d_q_gain"]), full(lw["w_uq"]), full(lw["d_kv_gain"]), full(lw["w_uk"]),
                  full(lw["w_uv"])],
        out_specs=a_specs + [out_spec[kind](w) for w, kind in outs],
        out_shape=a_shapes + [jax.ShapeDtypeStruct(out_shape[kind](w), BF16) for w, kind in outs],
        compiler_params=_cparams(1),
        name="proj",
    )(x2d, lw["norm_mix"], lw["w_in"], lw["b_q_gain"], lw["b_k_gain"], tabs["seg"], tabs["rope_b"],
      tabs["rope_d"], lw["d_q_gain"], lw["w_uq"], lw["d_kv_gain"], lw["w_uk"], lw["w_uv"])


def _attn_a_kernel(x_ref, tab_ref, o_ref, st_ref, *, n, n_res):
    nblk = n // QBLK_A
    lane = lax.broadcasted_iota(jnp.int32, (QBLK_A, LANES), 1)
    ones = jnp.ones((2 * QBLK_A, LANES), BF16)

    def body(i, carry, res):
        q0, k0, v0 = res * A_WIDTH, res * A_WIDTH + 2 * GROUP_WIDTH, res * A_WIDTH + 3 * GROUP_WIDTH
        rows = pl.ds(pl.multiple_of(i * QBLK_A, QBLK_A), QBLK_A)
        win = pl.ds(pl.multiple_of(jnp.clip(i * QBLK_A - BAND_HALF, 0, n - 2 * QBLK_A), BAND_HALF), 2 * QBLK_A)
        var = jnp.where(i == 0, 1, jnp.where(i == nblk - 1, 2, 0))

        def scores(hd):
            pair = (hd // 2) * LANES
            return _dot_nt(x_ref[rows, q0 + hd * LANES:q0 + (hd + 1) * LANES],
                           x_ref[win, k0 + pair:k0 + pair + LANES]) + tab_ref[var, hd]

        st = jnp.zeros((QBLK_A, LANES), F32)
        s_next = scores(0)
        for hd in range(N_HEADS):
            s = s_next
            if hd + 1 < N_HEADS:
                s_next = scores(hd + 1)
            pair = (hd // 2) * LANES
            m = jnp.max(s, axis=-1, keepdims=True)
            e = jnp.exp2(s - m).astype(BF16)
            pv = _dot(e, jnp.concatenate([x_ref[win, v0 + pair:v0 + pair + LANES], ones], axis=1))
            l = pv[:, LANES:]
            o_h = pv[:, :LANES] / l
            st = jnp.where(lane == hd, m + jnp.log2(l), st)
            if hd % 2 == 0:
                o_even = o_h
            else:
                o0 = res * GROUP_WIDTH + pair
                o_ref[rows, o0:o0 + LANES] = jnp.where(lane < HEAD_DIM, o_even, o_h)
        st_ref[rows, res * LANES:(res + 1) * LANES] = st
        return carry

    for res in range(n_res):
        lax.fori_loop(0, nblk, functools.partial(body, res=res), 0, unroll=min(UNROLL_A, nblk))


def _attn_a(a_dil, tab, nb, dil):
    n = SEQ // dil
    n_res = RES_PER_STEP_A[dil]
    view = a_dil.reshape(nb, n, dil * A_WIDTH)
    o, st = pl.pallas_call(
        functools.partial(_attn_a_kernel, n=n, n_res=n_res),
        grid=(nb, dil // n_res),
        in_specs=[pl.BlockSpec((None, n, n_res * A_WIDTH), lambda b, r: (b, 0, r)),
                  pl.BlockSpec(tab.shape, lambda b, r: (0, 0, 0, 0))],
        out_specs=[pl.BlockSpec((None, n, n_res * GROUP_WIDTH), lambda b, r: (b, 0, r)),
                   pl.BlockSpec((None, n, n_res * LANES), lambda b, r: (b, 0, r))],
        out_shape=[jax.ShapeDtypeStruct((nb, n, dil * GROUP_WIDTH), F32),
                   jax.ShapeDtypeStruct((nb, n, dil * LANES), F32)],
        compiler_params=_cparams(2),
        name=f"attn_a_d{dil}",
    )(view, tab)
    return o.reshape(nb * n, dil * GROUP_WIDTH), st.reshape(nb * n, dil * LANES)


def _flash(problems):
    n_chunks = SEQ // TK_DENSE
    ones = jnp.ones((ONES_ROWS, TK_DENSE), BF16)
    n_sub = TK_DENSE // SUB_DENSE
    state = [None] * len(problems)

    def scores(c):
        return [[_dot(k_at(c)[j * SUB_DENSE:(j + 1) * SUB_DENSE], q_t) for j in range(n_sub)]
                for q_t, k_at, _ in problems]

    s_cur = scores(0)
    for c in range(n_chunks):
        if c + 1 < n_chunks:
            s_next = scores(c + 1)
        for i, (subs, (_, _, v_t_at)) in enumerate(zip(s_cur, problems)):
            col_max = functools.reduce(jnp.maximum, [jnp.max(s, axis=0, keepdims=True) for s in subs])
            m_new = col_max if c == 0 else jnp.maximum(state[i][0], col_max)
            v_ext = jnp.concatenate([v_t_at(c), ones], axis=0)
            acc = None if c == 0 else jnp.exp2(state[i][0] - m_new) * state[i][1]
            for j, s in enumerate(subs):
                part = _dot(v_ext[:, j * SUB_DENSE:(j + 1) * SUB_DENSE], jnp.exp2(s - m_new).astype(BF16))
                acc = part if acc is None else acc + part
            state[i] = (m_new, acc)
        s_cur = s_next
    outs = []
    for (_, _, v_t_at), (_, acc) in zip(problems, state):
        dv = acc.shape[0] - ONES_ROWS
        outs.append(acc[:dv] / acc[dv:dv + 1])
    return outs


def _chunk(c):
    return pl.ds(c * TK_DENSE, TK_DENSE)


def _store_blocks(o_ref, block_out):
    def body(jb, carry):
        o_ref[pl.ds(pl.multiple_of(jb * QBLK_T, QBLK_T), QBLK_T), :] = block_out(jb)
        return carry

    lax.fori_loop(0, QBLKS_PER_STEP, body, 0)


def _attn_b_kernel(qt_ref, k_ref, vt_ref, o_ref):
    def block_out(jb):
        problems = []
        for hd in range(N_HEADS):
            g = hd // 2
            problems.append((qt_ref[jb, hd * LANES:(hd + 1) * LANES, :],
                             lambda c, g=g: k_ref[_chunk(c), g * LANES:(g + 1) * LANES],
                             lambda c, g=g: vt_ref[g * HEAD_DIM:(g + 1) * HEAD_DIM, _chunk(c)]))
        return jnp.concatenate(_flash(problems), axis=0).T

    _store_blocks(o_ref, block_out)


def _attn_b(b_qt, b_k, b_vt, nb):
    tq = QBLK_T * QBLKS_PER_STEP
    nq = SEQ // tq
    o = pl.pallas_call(
        _attn_b_kernel,
        grid=(nb, nq),
        in_specs=[pl.BlockSpec((QBLKS_PER_STEP, N_HEADS * LANES, QBLK_T), lambda b, i: (b * nq + i, 0, 0)),
                  pl.BlockSpec((None, SEQ, 2 * LANES), lambda b, i: (b, 0, 0)),
                  pl.BlockSpec((2 * HEAD_DIM, SEQ), lambda b, i: (0, b))],
        out_specs=pl.BlockSpec((None, tq, GROUP_WIDTH), lambda b, i: (b, i, 0)),
        out_shape=jax.ShapeDtypeStruct((nb, SEQ, GROUP_WIDTH), F32),
        compiler_params=_cparams(2),
        name="attn_b",
    )(b_qt, b_k.reshape(nb, SEQ, 2 * LANES), b_vt)
    return o.reshape(nb * SEQ, GROUP_WIDTH)


def _attn_d_kernel(qt_ref, k_ref, vt_ref, o_ref):
    def block_out(jb):
        problems = []
        for j in range(2):
            problems.append((qt_ref[jb, j * LANES:(j + 1) * LANES, :],
                             lambda c, j=j: k_ref[_chunk(c), j * LANES:(j + 1) * LANES],
                             lambda c, j=j: vt_ref[j * D_V:(j + 1) * D_V, _chunk(c)]))
        return jnp.concatenate(_flash(problems), axis=0).T

    _store_blocks(o_ref, block_out)


def _attn_d(d_qt, d_k, d_vt, nb):
    tq = QBLK_T * QBLKS_PER_STEP
    nq = SEQ // tq
    o = pl.pallas_call(
        _attn_d_kernel,
        grid=(nb, 2, nq),
        in_specs=[pl.BlockSpec((QBLKS_PER_STEP, 2 * LANES, QBLK_T), lambda b, p, i: (b * nq + i, p, 0)),
                  pl.BlockSpec((None, SEQ, 2 * LANES), lambda b, p, i: (b, 0, p)),
                  pl.BlockSpec((2 * D_V, SEQ), lambda b, p, i: (p, b))],
        out_specs=pl.BlockSpec((None, tq, LANES), lambda b, p, i: (b, i, p)),
        out_shape=jax.ShapeDtypeStruct((nb, SEQ, GROUP_WIDTH), F32),
        compiler_params=_cparams(3),
        name="attn_d",
    )(d_qt, d_k.reshape(nb, SEQ, N_HEADS * LANES), d_vt)
    return o.reshape(nb * SEQ, GROUP_WIDTH)


def _attn_c_kernel(qt_ref, k_ref, vt_ref, tab_ref, o_ref):
    n_groups = GRID_ROWS // ROWS_C
    ones = jnp.ones((ONES_ROWS, WIN_ROWS_C * GRID_W), BF16)

    def block_out(jb):
        i = pl.program_id(1) * QBLKS_PER_STEP + jb
        var = jnp.where(i == 0, 1, jnp.where(i == n_groups - 1, 2, 0))
        k0 = pl.multiple_of(jnp.clip(i * ROWS_C - NA_ROWS // 2, 0, GRID_ROWS - WIN_ROWS_C) * GRID_W,
                            ROWS_C * GRID_W)
        keys = pl.ds(k0, WIN_ROWS_C * GRID_W)

        def scores(hd):
            pair = slice((hd // 2) * LANES, (hd // 2 + 1) * LANES)
            return _dot(k_ref[keys, pair], qt_ref[jb, hd * LANES:(hd + 1) * LANES, :]) + tab_ref[var, hd]

        outs = []
        s_next = scores(0)
        for hd in range(N_HEADS):
            s = s_next
            if hd + 1 < N_HEADS:
                s_next = scores(hd + 1)
            p = jnp.exp2(s - jnp.max(s, axis=0, keepdims=True)).astype(BF16)
            v_ext = jnp.concatenate([vt_ref[hd * HEAD_DIM:(hd + 1) * HEAD_DIM, keys], ones], axis=0)
            acc = _dot(v_ext, p)
            outs.append(acc[:HEAD_DIM] / acc[HEAD_DIM:HEAD_DIM + 1])
        return jnp.concatenate(outs, axis=0).T

    _store_blocks(o_ref, block_out)


def _attn_c(c_qt, c_k, c_vt, tab, nb):
    assert QBLK_T == ROWS_C * GRID_W
    tq = QBLK_T * QBLKS_PER_STEP
    nq = SEQ // tq
    o = pl.pallas_call(
        _attn_c_kernel,
        grid=(nb, nq),
        in_specs=[pl.BlockSpec((QBLKS_PER_STEP, N_HEADS * LANES, QBLK_T), lambda b, i: (b * nq + i, 0, 0)),
                  pl.BlockSpec((None, SEQ, GROUP_WIDTH), lambda b, i: (b, 0, 0)),
                  pl.BlockSpec((GROUP_WIDTH, SEQ), lambda b, i: (0, b)),
                  pl.BlockSpec(tab.shape, lambda b, i: (0, 0, 0, 0), pipeline_mode=pl.Buffered(1))],
        out_specs=pl.BlockSpec((None, tq, GROUP_WIDTH), lambda b, i: (b, i, 0)),
        out_shape=jax.ShapeDtypeStruct((nb, SEQ, GROUP_WIDTH), F32),
        compiler_params=_cparams(2),
        name="attn_c",
    )(c_qt, c_k.reshape(nb, SEQ, GROUP_WIDTH), c_vt, tab)
    return o.reshape(nb * SEQ, GROUP_WIDTH)


def _out_ffn_kernel(x_ref, oa1_ref, sa1_ref, oa2_ref, sa2_ref, oa3_ref, sa3_ref, ob_ref, oc_ref, od_ref,
                    expand_ref, og_ref, wout_ref, nf_ref, wg_ref, wu_ref, wd_ref, fin_ref, y_ref,
                    o_scr, s_scr, *, final):
    for j, (dil, o_ref, s_ref) in enumerate(((DIL_MID, oa2_ref, sa2_ref), (DIL_FAR, oa3_ref, sa3_ref))):
        for r in range(dil):
            dst = pl.ds(r, TM_FFN // dil, stride=dil)
            for s in range(GROUP_WIDTH // LANES):
                c0 = r * GROUP_WIDTH + s * LANES
                o_scr[2 * j + s, dst, :] = o_ref[:, c0:c0 + LANES]
            s_scr[j, dst, :] = s_ref[:, r * LANES:(r + 1) * LANES]
    oa = [oa1_ref[...]] + [jnp.concatenate([o_scr[2 * j], o_scr[2 * j + 1]], axis=1) for j in range(2)]

    lses = [sa1_ref[...], s_scr[0], s_scr[1]]
    mx = jnp.maximum(jnp.maximum(lses[0], lses[1]), lses[2])
    es = [jnp.exp2(s - mx) for s in lses]
    den = es[0] + es[1] + es[2]
    expand = expand_ref[...]

    def per_head(w):
        hi = w.astype(BF16)
        lo = (w - hi.astype(F32)).astype(BF16)
        return _dot(hi, expand) + _dot(lo, expand)

    o_a = per_head(es[0] / den) * oa[0] + per_head(es[1] / den) * oa[1] + per_head(es[2] / den) * oa[2]

    x = x_ref[...]
    groups = (o_a, ob_ref[...], oc_ref[...], od_ref[...])
    normed = [_rms(o, og_ref[:, g * GROUP_WIDTH:(g + 1) * GROUP_WIDTH]).astype(BF16) for g, o in enumerate(groups)]
    x1 = x + _dot(jnp.concatenate(normed, axis=1), wout_ref[...])

    h = _rms(x1, nf_ref[...]).astype(BF16)

    def gate_up(c):
        cols = slice(FF_BOUNDS[c], FF_BOUNDS[c + 1])
        gate = _dot(h, wg_ref[:, cols])
        up = _dot(h, wu_ref[:, cols])
        return (gate * (1.0 / (1.0 + jnp.exp(-gate))) * up).astype(BF16)

    n_ff = len(FF_BOUNDS) - 1
    act = gate_up(0)
    ff = None
    for c in range(n_ff):
        act_next = gate_up(c + 1) if c + 1 < n_ff else None
        part = _dot(act, wd_ref[FF_BOUNDS[c]:FF_BOUNDS[c + 1], :])
        ff = part if ff is None else ff + part
        act = act_next
    x2 = x1 + ff
    if final:
        x2 = _rms(x2, fin_ref[...])
    y_ref[...] = x2


def _out_ffn(x2d, attn_outs, lw, expand, final_norm, final):
    t = x2d.shape[0]
    row = lambda i: (i, 0)

    def resident(a):
        return pl.BlockSpec(a.shape, lambda i: (0, 0), pipeline_mode=pl.Buffered(1))

    (oa1, sa1), (oa2, sa2), (oa3, sa3), ob, oc, od = attn_outs
    wide = pl.BlockSpec((TM_FFN, GROUP_WIDTH), row)
    a_specs = [spec for _, d in DILATED_PATTERNS
               for spec in (pl.BlockSpec((TM_FFN // d, d * GROUP_WIDTH), row),
                            pl.BlockSpec((TM_FFN // d, d * LANES), row))]
    weights = (expand, lw["out_gain"], lw["w_out"], lw["norm_ffn"], lw["w_gate"], lw["w_up"], lw["w_down"],
               final_norm)
    return pl.pallas_call(
        functools.partial(_out_ffn_kernel, final=final),
        grid=(t // TM_FFN,),
        in_specs=[pl.BlockSpec((TM_FFN, D_MODEL), row)] + a_specs + [wide, wide, wide]
                 + [resident(w) for w in weights],
        out_specs=pl.BlockSpec((TM_FFN, D_MODEL), row),
        out_shape=jax.ShapeDtypeStruct((t, D_MODEL), F32),
        scratch_shapes=[pltpu.VMEM((2 * GROUP_WIDTH // LANES, TM_FFN, LANES), F32),
                        pltpu.VMEM((2, TM_FFN, LANES), F32)],
        compiler_params=_cparams(1),
        name="out_ffn",
    )(x2d, oa1, sa1, oa2, sa2, oa3, sa3, ob, oc, od, *weights)


def _rope_angles(pos, dim):
    inv = 1.0 / (ROPE_THETA ** (jnp.arange(0, dim, 2, dtype=F32) / dim))
    return pos.astype(F32)[:, None] * inv[None, :]


def _t5_bucket(rel):
    nb = T5_BUCKETS // 2
    max_exact = nb // 2
    n = jnp.abs(rel)
    n_f = jnp.maximum(n, max_exact).astype(F32)
    large = max_exact + (jnp.log(n_f / max_exact) / math.log(T5_MAX_DIST / max_exact)
                         * (nb - max_exact)).astype(jnp.int32)
    large = jnp.minimum(large, nb - 1)
    return jnp.where(rel > 0, nb, 0) + jnp.where(n < max_exact, n, large)


def _tables(t5_bias):
    t = jnp.arange(SEQ, dtype=jnp.int32)
    z16 = jnp.zeros((SEQ, 16), F32)

    def trio(parts):
        return jnp.stack([jnp.concatenate(p, axis=1) for p in parts])

    ang_r = _rope_angles(t // GRID_W, HEAD_DIM // 2)
    ang_c = _rope_angles(t % GRID_W, HEAD_DIM // 2)
    cr, sr, cc, sc = jnp.cos(ang_r), jnp.sin(ang_r), jnp.cos(ang_c), jnp.sin(ang_c)
    rope_b = trio(([cr, cr, cc, cc] * 2, [-sr, z16, -sc, z16] * 2, [z16, sr, z16, sc] * 2))

    ang_t = _rope_angles(t, D_ROPE)
    ct, st = jnp.cos(ang_t), jnp.sin(ang_t)
    ones64, z64, z32 = jnp.ones((SEQ, 64), F32), jnp.zeros((SEQ, 64), F32), jnp.zeros((SEQ, 32), F32)
    rope_d = trio(([ones64, ct, ct, z32], [z64, -st, z16, z32], [z64, z16, st, z32]))

    seg = jnp.kron(jnp.eye(2, dtype=F32), jnp.full((HEAD_DIM, HEAD_DIM), 1.0 / HEAD_DIM, F32)).astype(BF16)
    expand = jnp.zeros((LANES, GROUP_WIDTH), F32).at[:N_HEADS].set(
        jnp.kron(jnp.eye(N_HEADS, dtype=F32), jnp.ones((1, HEAD_DIM), F32))).astype(BF16)

    period = 4 * QBLK_A
    u = np.arange(period)
    rel = np.where(u < period - QBLK_A, u, u - period)
    band = []
    for _, dil in DILATED_PATTERNS:
        off = jnp.arange(-BAND_HALF, BAND_HALF + 1, dtype=jnp.int32) * dil
        bias = t5_bias[_t5_bucket(off)].T.astype(F32)
        variants = []
        for shift in (-BAND_HALF, 0, -2 * BAND_HALF):
            delta = rel + shift
            inside = np.abs(delta) <= BAND_HALF
            diag = jnp.where(inside[None], bias[:, np.clip(delta + BAND_HALF, 0, 2 * BAND_HALF)] * LOG2E,
                             NEG_INF)
            variants.append(_toeplitz(diag, QBLK_A, 2 * QBLK_A))
        band.append(jnp.stack(variants))
    return {"rope_b": rope_b, "rope_d": rope_d, "seg": seg, "expand": expand, "band": band}


def _toeplitz(w, n_rows, n_cols):
    period = w.shape[-1]
    tiled = jnp.tile(w, (1,) * (w.ndim - 1) + (n_rows,))
    return tiled[..., :n_rows * (period - 1)].reshape(w.shape[:-1] + (n_rows, period - 1))[..., :n_cols]


def _na_table(rpb):
    cols = np.arange(GRID_W)
    cs = np.clip(cols - NA_COLS // 2, 0, GRID_W - NA_COLS)
    inside = (cols[None, :] >= cs[:, None]) & (cols[None, :] < cs[:, None] + NA_COLS)
    period = 2 * GRID_W
    u = np.arange(period)
    dc = np.where(u < GRID_W, u, u - period) + NA_COLS - 1
    diag = jnp.where(((dc >= 0) & (dc <= 2 * NA_COLS - 2))[None, None],
                     rpb.astype(F32)[:, :, np.clip(dc, 0, 2 * NA_COLS - 2)], NEG_INF)
    vals = jnp.where(inside[None, None], _toeplitz(diag, GRID_W, GRID_W), NEG_INF)
    vals = jnp.swapaxes(vals, 2, 3) * LOG2E
    masked = jnp.full((N_HEADS, GRID_W, GRID_W), NEG_INF, F32)
    tabs = []
    for lo, step, base in ((0, 1, NA_ROWS // 2 - 1), (0, 0, NA_ROWS - 1), (NA_ROWS // 2, 0, -1)):
        rows = []
        for w in range(WIN_ROWS_C):
            per_a = [vals[:, w - a + base] if lo <= w - a * step < lo + NA_ROWS else masked
                     for a in range(ROWS_C)]
            rows.append(jnp.concatenate(per_a, axis=-1))
        tabs.append(jnp.concatenate(rows, axis=1))
    return jnp.stack(tabs)


def _layer_weights(l, norm_mix, w_in, b_q_gain, b_k_gain, d_q_gain, d_w_uq, d_kv_gain, d_w_ukv,
                   out_gain, w_out, norm_ffn, w_gate, w_up, w_down):
    d_in = w_in.shape[-1]
    kr_lo = d_in - D_ROPE
    w_in_p = jnp.zeros((D_MODEL, D_IN_PAD), F32)
    w_in_p = w_in_p.at[:, :kr_lo].set(w_in[l][:, :kr_lo])
    w_in_p = w_in_p.at[:, kr_lo + D_NOPE:kr_lo + D_NOPE + D_ROPE].set(w_in[l][:, kr_lo:])
    dqk = D_NOPE + D_ROPE
    w_uq = jnp.zeros((D_Q_LORA, N_HEADS, LANES), F32).at[:, :, :dqk].set(
        d_w_uq[l].reshape(D_Q_LORA, N_HEADS, dqk)).reshape(D_Q_LORA, N_HEADS * LANES)
    ukv = d_w_ukv[l].reshape(D_KV_LORA, N_HEADS, D_NOPE + D_V)
    w_uk = jnp.zeros((D_KV_LORA, N_HEADS, LANES), F32).at[:, :, :D_NOPE].set(
        ukv[:, :, :D_NOPE]).reshape(D_KV_LORA, N_HEADS * LANES)
    w_uv = ukv[:, :, D_NOPE:].reshape(D_KV_LORA, N_HEADS * D_V)
    return {
        "norm_mix": norm_mix[l][None, :], "w_in": w_in_p.astype(BF16),
        "b_q_gain": jnp.tile(b_q_gain[l], 2)[None, :], "b_k_gain": jnp.tile(b_k_gain[l], 2)[None, :],
        "d_q_gain": d_q_gain[l][None, :], "w_uq": w_uq.astype(BF16),
        "d_kv_gain": d_kv_gain[l][None, :], "w_uk": w_uk.astype(BF16), "w_uv": w_uv.astype(BF16),
        "out_gain": out_gain[l][None, :], "w_out": w_out[l].astype(BF16), "norm_ffn": norm_ffn[l][None, :],
        "w_gate": w_gate[l].astype(BF16), "w_up": w_up[l].astype(BF16), "w_down": w_down[l].astype(BF16),
    }


def _trunk(x, layers, tabs, na_tabs, final_norm):
    nb = x.shape[0]
    x2d = x.reshape(nb * SEQ, D_MODEL)
    for l, lw in enumerate(layers):
        a1, a4, a16, b_qt, b_k, b_vt, c_qt, c_k, c_vt, d_qt, d_k, d_vt = _proj(x2d, lw, tabs)
        outs_a = tuple(_attn_a(a_dil, tabs["band"][p], nb, dil)
                       for p, (a_dil, (_, dil)) in enumerate(zip((a1, a4, a16), DILATED_PATTERNS)))
        o_b = _attn_b(b_qt, b_k, b_vt, nb)
        o_c = _attn_c(c_qt, c_k, c_vt, na_tabs[l], nb)
        o_d = _attn_d(d_qt, d_k, d_vt, nb)
        x2d = _out_ffn(x2d, outs_a + (o_b, o_c, o_d), lw, tabs["expand"], final_norm, final=(l == DEPTH - 1))
    return x2d.reshape(nb, SEQ, D_MODEL)


def kernel(x_prompt, x_sample, t5_bias, norm_mix, w_in, b_q_gain, b_k_gain, c_rpb, d_q_gain, d_w_uq,
           d_kv_gain, d_w_ukv, out_gain, w_out, norm_ffn, w_gate, w_up, w_down, final_norm):
    tabs = _tables(t5_bias)
    na_tabs = [_na_table(c_rpb[l]) for l in range(DEPTH)]
    layers = [_layer_weights(l, norm_mix, w_in, b_q_gain, b_k_gain, d_q_gain, d_w_uq, d_kv_gain, d_w_ukv,
                             out_gain, w_out, norm_ffn, w_gate, w_up, w_down) for l in range(DEPTH)]
    fin = final_norm[None, :]
    return (_trunk(x_prompt, layers, tabs, na_tabs, fin), _trunk(x_sample, layers, tabs, na_tabs, fin))
```

```python
import functools
import math

import jax
import jax.numpy as jnp
import numpy as np
from jax import lax
from jax.experimental import pallas as pl
from jax.experimental.pallas import tpu as pltpu

F32 = jnp.float32
BF16 = jnp.bfloat16

D_MODEL = 1024
SEQ = 4096
DEPTH = 2
HEAD_DIM = 64
N_HEADS = 4
GROUP_WIDTH = N_HEADS * HEAD_DIM
DILATED_PATTERNS = ((128, 1), (512, 4), (2048, 16))
BAND_HALF = 64
DIL_MID, DIL_FAR = DILATED_PATTERNS[1][1], DILATED_PATTERNS[2][1]
T5_BUCKETS = 32
T5_MAX_DIST = 1024
GRID_W = 64
GRID_ROWS = SEQ // GRID_W
NA_ROWS = 8
NA_COLS = 16
D_Q_LORA = 256
D_KV_LORA = 128
D_NOPE = 64
D_ROPE = 32
D_V = 64
ROPE_THETA = 10000.0
D_FF = 2816
RMS_EPS = 1e-6
NEG_INF = -1e30

LANES = 128
D_IN_PAD = 2560
COL_A, COL_B, COL_C, COL_D = 0, 768, 1280, 2048
A_WIDTH = 1024
ATTN_SCALE = HEAD_DIM ** -0.5
MLA_SCALE = (D_NOPE + D_ROPE) ** -0.5
LOG2E = math.log2(math.e)

TM_PROJ = 512
PROJ_SPLIT = 2
QBLK_T = TM_PROJ // PROJ_SPLIT
QBLKS_PER_STEP = 4
QBLK_A = 128
RES_PER_STEP_A = {1: 1, 4: 4, 16: 8}
UNROLL_A = 8
TK_DENSE = 512
SUB_DENSE = 256
ONES_ROWS = 16
ROWS_C = 4
WIN_ROWS_C = 12
TM_FFN = 512
FF_BOUNDS = (0, 768, 1536, 2304, D_FF)
VMEM_LIMIT = 56 * 1024 * 1024


def _cparams(n_axes):
    return pltpu.CompilerParams(dimension_semantics=("arbitrary",) * n_axes,
                                vmem_limit_bytes=VMEM_LIMIT)


def _rms(x, g):
    return x * lax.rsqrt(jnp.mean(x * x, axis=-1, keepdims=True) + RMS_EPS) * g


def _dot(a, b):
    return jnp.dot(a, b, preferred_element_type=F32)


def _dot_nt(a, b):
    return lax.dot_general(a, b, (((1,), (1,)), ((), ())), preferred_element_type=F32)


def _rope(y, tab):
    return y * tab[0] + pltpu.roll(y, LANES - 16, 1) * tab[1] + pltpu.roll(y, 16, 1) * tab[2]


def _proj_kernel(x_ref, gmix_ref, w_ref, bqg_ref, bkg_ref, seg_ref, tabb_ref, tabd_ref,
                 dqg_ref, wuq_ref, dkvg_ref, wuk_ref, wuv_ref,
                 a_ref, a4_ref, a16_ref, bqt_ref, bk_ref, bvt_ref, cqt_ref, ck_ref, cvt_ref, dqt_ref, dk_ref,
                 dvt_ref, a_scr, a4_scr):
    half = TM_PROJ // PROJ_SPLIT
    low_half = lax.broadcasted_iota(jnp.int32, (half, LANES), 1) < HEAD_DIM
    zeros_half = jnp.zeros((HEAD_DIM, half), BF16)
    seg = seg_ref[...]

    def main(rows):
        h = _rms(x_ref[rows, :], gmix_ref[...]).astype(BF16)
        return [_dot(h, w_ref[:, lo:hi]) for lo, hi in ((COL_A, COL_B), (COL_B, COL_C), (COL_C, COL_D),
                                                        (COL_D, D_IN_PAD))]

    def finish(jh, rows, pa, pb, pc, pd):
        slabs = []
        for hd in range(N_HEADS):
            pair = pa[:, (hd // 2) * LANES:(hd // 2 + 1) * LANES] * (ATTN_SCALE * LOG2E)
            own = low_half if hd % 2 == 0 else jnp.logical_not(low_half)
            slabs.append(jnp.where(own, pair, 0.0))
        slabs += [pa[:, GROUP_WIDTH + j * LANES:GROUP_WIDTH + (j + 1) * LANES] for j in range(4)]
        for s, val in enumerate(slabs):
            a_scr[s, rows, :] = val
            a_ref[rows, s * LANES:(s + 1) * LANES] = val.astype(BF16)

        cqt = (pc[:, :GROUP_WIDTH] * (ATTN_SCALE * LOG2E)).T.astype(BF16)
        for hd in range(N_HEADS):
            own = hd * LANES + (hd % 2) * HEAD_DIM
            other = hd * LANES + (1 - hd % 2) * HEAD_DIM
            cqt_ref[jh, own:own + HEAD_DIM, :] = cqt[hd * HEAD_DIM:(hd + 1) * HEAD_DIM]
            cqt_ref[jh, other:other + HEAD_DIM, :] = zeros_half
        ck_ref[rows, :] = pc[:, GROUP_WIDTH:2 * GROUP_WIDTH].astype(BF16)
        cvt_ref[:, rows] = pc[:, 2 * GROUP_WIDTH:].T.astype(BF16)

        tabb = tabb_ref[:, rows, :]

        def head_norm_rope(chunk, gain):
            sq = chunk * chunk
            hi = sq.astype(BF16)
            lo = (sq - hi.astype(F32)).astype(BF16)
            ms = _dot(hi, seg) + _dot(lo, seg)
            return _rope(chunk * lax.rsqrt(ms + RMS_EPS) * gain, tabb)

        for c in range(2):
            q = head_norm_rope(pb[:, c * LANES:(c + 1) * LANES], bqg_ref[...])
            qt = (q * (ATTN_SCALE * LOG2E)).T.astype(BF16)
            for j in range(2):
                r0 = (2 * c + j) * LANES
                bqt_ref[jh, r0:r0 + HEAD_DIM, :] = qt[j * HEAD_DIM:(j + 1) * HEAD_DIM]
                bqt_ref[jh, r0 + HEAD_DIM:r0 + LANES, :] = zeros_half
        kb = head_norm_rope(pb[:, 2 * LANES:3 * LANES], bkg_ref[...])
        bk_ref[rows, :LANES] = kb.astype(BF16)
        bk_ref[rows, LANES:] = pltpu.roll(kb, HEAD_DIM, 1).astype(BF16)
        bvt_ref[:, rows] = pb[:, 3 * LANES:].T.astype(BF16)

        tabd = tabd_ref[:, rows, :]
        dq = _rms(pd[:, :D_Q_LORA], dqg_ref[...]).astype(BF16)
        qd = _dot(dq, wuq_ref[...])
        dkv = _rms(pd[:, D_Q_LORA:D_Q_LORA + D_KV_LORA], dkvg_ref[...]).astype(BF16)
        kn = _dot(dkv, wuk_ref[...])
        dvt_ref[:, rows] = _dot(dkv, wuv_ref[...]).T.astype(BF16)
        kr = _rope(pd[:, D_Q_LORA + D_KV_LORA:], tabd)
        for hd in range(N_HEADS):
            sl = slice(hd * LANES, (hd + 1) * LANES)
            dqt_ref[jh, sl, :] = (_rope(qd[:, sl], tabd) * (MLA_SCALE * LOG2E)).T.astype(BF16)
            dk_ref[rows, sl] = (kn[:, sl] + kr).astype(BF16)

    row_sets = [slice(j * half, (j + 1) * half) for j in range(PROJ_SPLIT)]
    mains = [main(rows) for rows in row_sets]
    for jh, (rows, (pa, pb, pc, pd)) in enumerate(zip(row_sets, mains)):
        finish(jh, rows, pa, pb, pc, pd)

    n_slabs = A_WIDTH // LANES
    for r in range(DIL_MID):
        for s in range(n_slabs):
            val = a_scr[s, pl.ds(r, TM_PROJ // DIL_MID, stride=DIL_MID), :]
            a4_scr[r * n_slabs + s] = val
            a4_ref[:, (r * n_slabs + s) * LANES:(r * n_slabs + s + 1) * LANES] = val.astype(BF16)
    step = DIL_FAR // DIL_MID
    for j in range(step):
        for s in range(DIL_MID * n_slabs):
            c0 = (j * DIL_MID * n_slabs + s) * LANES
            a16_ref[:, c0:c0 + LANES] = a4_scr[s, pl.ds(j, TM_PROJ // DIL_FAR, stride=step), :].astype(BF16)


def _proj(x2d, lw, tabs):
    t = x2d.shape[0]
    blocks_per_seq = SEQ // TM_PROJ
    row = lambda i: (i, 0)
    const2 = lambda i: (0, 0)
    pos3 = lambda i: (0, i % blocks_per_seq, 0)

    def full(a):
        return pl.BlockSpec(a.shape, const2)

    col = lambda i: (0, i)
    outs = ((512, "qblk"), (256, "row"), (128, "col"), (512, "qblk"), (256, "row"), (256, "col"),
            (512, "qblk"), (512, "row"), (256, "col"))
    out_spec = {"row": lambda w: pl.BlockSpec((TM_PROJ, w), row),
                "col": lambda w: pl.BlockSpec((w, TM_PROJ), col),
                "qblk": lambda w: pl.BlockSpec((PROJ_SPLIT, w, QBLK_T), lambda i: (i, 0, 0))}
    out_shape = {"row": lambda w: (t, w), "col": lambda w: (w, t), "qblk": lambda w: (t // QBLK_T, w, QBLK_T)}
    a_specs = [pl.BlockSpec((TM_PROJ // d, d * A_WIDTH), row) for _, d in DILATED_PATTERNS]
    a_shapes = [jax.ShapeDtypeStruct((t // d, d * A_WIDTH), BF16) for _, d in DILATED_PATTERNS]
    return pl.pallas_call(
        _proj_kernel,
        grid=(t // TM_PROJ,),
        scratch_shapes=[pltpu.VMEM((A_WIDTH // LANES, TM_PROJ, LANES), F32),
                        pltpu.VMEM((DIL_MID * A_WIDTH // LANES, TM_PROJ // DIL_MID, LANES), F32)],
        in_specs=[pl.BlockSpec((TM_PROJ, D_MODEL), row), full(lw["norm_mix"]), full(lw["w_in"]),
                  full(lw["b_q_gain"]), full(lw["b_k_gain"]), full(tabs["seg"]),
                  pl.BlockSpec((3, TM_PROJ, LANES), pos3), pl.BlockSpec((3, TM_PROJ, LANES), pos3),
                  full(lw["d_q_gain"]), full(lw["w_uq"]), full(lw["d_kv_gain"]), full(lw["w_uk"]),
                  full(lw["w_uv"])],
        out_specs=a_specs + [out_spec[kind](w) for w, kind in outs],
        out_shape=a_shapes + [jax.ShapeDtypeStruct(out_shape[kind](w), BF16) for w, kind in outs],
        compiler_params=_cparams(1),
        name="proj",
    )(x2d, lw["norm_mix"], lw["w_in"], lw["b_q_gain"], lw["b_k_gain"], tabs["seg"], tabs["rope_b"],
      tabs["rope_d"], lw["d_q_gain"], lw["w_uq"], lw["d_kv_gain"], lw["w_uk"], lw["w_uv"])


def _attn_a_kernel(x_ref, tab_ref, o_ref, st_ref, *, n, n_res):
    nblk = n // QBLK_A
    lane = lax.broadcasted_iota(jnp.int32, (QBLK_A, LANES), 1)
    ones = jnp.ones((2 * QBLK_A, LANES), BF16)

    def body(i, carry, res):
        q0, k0, v0 = res * A_WIDTH, res * A_WIDTH + 2 * GROUP_WIDTH, res * A_WIDTH + 3 * GROUP_WIDTH
        rows = pl.ds(pl.multiple_of(i * QBLK_A, QBLK_A), QBLK_A)
        win = pl.ds(pl.multiple_of(jnp.clip(i * QBLK_A - BAND_HALF, 0, n - 2 * QBLK_A), BAND_HALF), 2 * QBLK_A)
        var = jnp.where(i == 0, 1, jnp.where(i == nblk - 1, 2, 0))

        def scores(hd):
            pair = (hd // 2) * LANES
            return _dot_nt(x_ref[rows, q0 + hd * LANES:q0 + (hd + 1) * LANES],
                           x_ref[win, k0 + pair:k0 + pair + LANES]) + tab_ref[var, hd]

        st = jnp.zeros((QBLK_A, LANES), F32)
        s_next = scores(0)
        for hd in range(N_HEADS):
            s = s_next
            if hd + 1 < N_HEADS:
                s_next = scores(hd + 1)
            pair = (hd // 2) * LANES
            m = jnp.max(s, axis=-1, keepdims=True)
            e = jnp.exp2(s - m).astype(BF16)
            pv = _dot(e, jnp.concatenate([x_ref[win, v0 + pair:v0 + pair + LANES], ones], axis=1))
            l = pv[:, LANES:]
            o_h = pv[:, :LANES] / l
            st = jnp.where(lane == hd, m + jnp.log2(l), st)
            if hd % 2 == 0:
                o_even = o_h
            else:
                o0 = res * GROUP_WIDTH + pair
                o_ref[rows, o0:o0 + LANES] = jnp.where(lane < HEAD_DIM, o_even, o_h)
        st_ref[rows, res * LANES:(res + 1) * LANES] = st
        return carry

    for res in range(n_res):
        lax.fori_loop(0, nblk, functools.partial(body, res=res), 0, unroll=min(UNROLL_A, nblk))


def _attn_a(a_dil, tab, nb, dil):
    n = SEQ // dil
    n_res = RES_PER_STEP_A[dil]
    view = a_dil.reshape(nb, n, dil * A_WIDTH)
    o, st = pl.pallas_call(
        functools.partial(_attn_a_kernel, n=n, n_res=n_res),
        grid=(nb, dil // n_res),
        in_specs=[pl.BlockSpec((None, n, n_res * A_WIDTH), lambda b, r: (b, 0, r)),
                  pl.BlockSpec(tab.shape, lambda b, r: (0, 0, 0, 0))],
        out_specs=[pl.BlockSpec((None, n, n_res * GROUP_WIDTH), lambda b, r: (b, 0, r)),
                   pl.BlockSpec((None, n, n_res * LANES), lambda b, r: (b, 0, r))],
        out_shape=[jax.ShapeDtypeStruct((nb, n, dil * GROUP_WIDTH), F32),
                   jax.ShapeDtypeStruct((nb, n, dil * LANES), F32)],
        compiler_params=_cparams(2),
        name=f"attn_a_d{dil}",
    )(view, tab)
    return o.reshape(nb * n, dil * GROUP_WIDTH), st.reshape(nb * n, dil * LANES)


def _flash(problems):
    n_chunks = SEQ // TK_DENSE
    ones = jnp.ones((ONES_ROWS, TK_DENSE), BF16)
    n_sub = TK_DENSE // SUB_DENSE
    state = [None] * len(problems)

    def scores(c):
        return [[_dot(k_at(c)[j * SUB_DENSE:(j + 1) * SUB_DENSE], q_t) for j in range(n_sub)]
                for q_t, k_at, _ in problems]

    s_cur = scores(0)
    for c in range(n_chunks):
        if c + 1 < n_chunks:
            s_next = scores(c + 1)
        for i, (subs, (_, _, v_t_at)) in enumerate(zip(s_cur, problems)):
            col_max = functools.reduce(jnp.maximum, [jnp.max(s, axis=0, keepdims=True) for s in subs])
            m_new = col_max if c == 0 else jnp.maximum(state[i][0], col_max)
            v_ext = jnp.concatenate([v_t_at(c), ones], axis=0)
            acc = None if c == 0 else jnp.exp2(state[i][0] - m_new) * state[i][1]
            for j, s in enumerate(subs):
                part = _dot(v_ext[:, j * SUB_DENSE:(j + 1) * SUB_DENSE], jnp.exp2(s - m_new).astype(BF16))
                acc = part if acc is None else acc + part
            state[i] = (m_new, acc)
        s_cur = s_next
    outs = []
    for (_, _, v_t_at), (_, acc) in zip(problems, state):
        dv = acc.shape[0] - ONES_ROWS
        outs.append(acc[:dv] / acc[dv:dv + 1])
    return outs


def _chunk(c):
    return pl.ds(c * TK_DENSE, TK_DENSE)


def _store_blocks(o_ref, block_out):
    def body(jb, carry):
        o_ref[pl.ds(pl.multiple_of(jb * QBLK_T, QBLK_T), QBLK_T), :] = block_out(jb)
        return carry

    lax.fori_loop(0, QBLKS_PER_STEP, body, 0)


def _attn_b_kernel(qt_ref, k_ref, vt_ref, o_ref):
    def block_out(jb):
        problems = []
        for hd in range(N_HEADS):
            g = hd // 2
            problems.append((qt_ref[jb, hd * LANES:(hd + 1) * LANES, :],
                             lambda c, g=g: k_ref[_chunk(c), g * LANES:(g + 1) * LANES],
                             lambda c, g=g: vt_ref[g * HEAD_DIM:(g + 1) * HEAD_DIM, _chunk(c)]))
        return jnp.concatenate(_flash(problems), axis=0).T

    _store_blocks(o_ref, block_out)


def _attn_b(b_qt, b_k, b_vt, nb):
    tq = QBLK_T * QBLKS_PER_STEP
    nq = SEQ // tq
    o = pl.pallas_call(
        _attn_b_kernel,
        grid=(nb, nq),
        in_specs=[pl.BlockSpec((QBLKS_PER_STEP, N_HEADS * LANES, QBLK_T), lambda b, i: (b * nq + i, 0, 0)),
                  pl.BlockSpec((None, SEQ, 2 * LANES), lambda b, i: (b, 0, 0)),
                  pl.BlockSpec((2 * HEAD_DIM, SEQ), lambda b, i: (0, b))],
        out_specs=pl.BlockSpec((None, tq, GROUP_WIDTH), lambda b, i: (b, i, 0)),
        out_shape=jax.ShapeDtypeStruct((nb, SEQ, GROUP_WIDTH), F32),
        compiler_params=_cparams(2),
        name="attn_b",
    )(b_qt, b_k.reshape(nb, SEQ, 2 * LANES), b_vt)
    return o.reshape(nb * SEQ, GROUP_WIDTH)


def _attn_d_kernel(qt_ref, k_ref, vt_ref, o_ref):
    def block_out(jb):
        problems = []
        for j in range(2):
            problems.append((qt_ref[jb, j * LANES:(j + 1) * LANES, :],
                             lambda c, j=j: k_ref[_chunk(c), j * LANES:(j + 1) * LANES],
                             lambda c, j=j: vt_ref[j * D_V:(j + 1) * D_V, _chunk(c)]))
        return jnp.concatenate(_flash(problems), axis=0).T

    _store_blocks(o_ref, block_out)


def _attn_d(d_qt, d_k, d_vt, nb):
    tq = QBLK_T * QBLKS_PER_STEP
    nq = SEQ // tq
    o = pl.pallas_call(
        _attn_d_kernel,
        grid=(nb, 2, nq),
        in_specs=[pl.BlockSpec((QBLKS_PER_STEP, 2 * LANES, QBLK_T), lambda b, p, i: (b * nq + i, p, 0)),
                  pl.BlockSpec((None, SEQ, 2 * LANES), lambda b, p, i: (b, 0, p)),
                  pl.BlockSpec((2 * D_V, SEQ), lambda b, p, i: (p, b))],
        out_specs=pl.BlockSpec((None, tq, LANES), lambda b, p, i: (b, i, p)),
        out_shape=jax.ShapeDtypeStruct((nb, SEQ, GROUP_WIDTH), F32),
        compiler_params=_cparams(3),
        name="attn_d",
    )(d_qt, d_k.reshape(nb, SEQ, N_HEADS * LANES), d_vt)
    return o.reshape(nb * SEQ, GROUP_WIDTH)


def _attn_c_kernel(qt_ref, k_ref, vt_ref, tab_ref, o_ref):
    n_groups = GRID_ROWS // ROWS_C
    ones = jnp.ones((ONES_ROWS, WIN_ROWS_C * GRID_W), BF16)

    def block_out(jb):
        i = pl.program_id(1) * QBLKS_PER_STEP + jb
        var = jnp.where(i == 0, 1, jnp.where(i == n_groups - 1, 2, 0))
        k0 = pl.multiple_of(jnp.clip(i * ROWS_C - NA_ROWS // 2, 0, GRID_ROWS - WIN_ROWS_C) * GRID_W,
                            ROWS_C * GRID_W)
        keys = pl.ds(k0, WIN_ROWS_C * GRID_W)

        def scores(hd):
            pair = slice((hd // 2) * LANES, (hd // 2 + 1) * LANES)
            return _dot(k_ref[keys, pair], qt_ref[jb, hd * LANES:(hd + 1) * LANES, :]) + tab_ref[var, hd]

        outs = []
        s_next = scores(0)
        for hd in range(N_HEADS):
            s = s_next
            if hd + 1 < N_HEADS:
                s_next = scores(hd + 1)
            p = jnp.exp2(s - jnp.max(s, axis=0, keepdims=True)).astype(BF16)
            v_ext = jnp.concatenate([vt_ref[hd * HEAD_DIM:(hd + 1) * HEAD_DIM, keys], ones], axis=0)
            acc = _dot(v_ext, p)
            outs.append(acc[:HEAD_DIM] / acc[HEAD_DIM:HEAD_DIM + 1])
        return jnp.concatenate(outs, axis=0).T

    _store_blocks(o_ref, block_out)


def _attn_c(c_qt, c_k, c_vt, tab, nb):
    assert QBLK_T == ROWS_C * GRID_W
    tq = QBLK_T * QBLKS_PER_STEP
    nq = SEQ // tq
    o = pl.pallas_call(
        _attn_c_kernel,
        grid=(nb, nq),
        in_specs=[pl.BlockSpec((QBLKS_PER_STEP, N_HEADS * LANES, QBLK_T), lambda b, i: (b * nq + i, 0, 0)),
                  pl.BlockSpec((None, SEQ, GROUP_WIDTH), lambda b, i: (b, 0, 0)),
                  pl.BlockSpec((GROUP_WIDTH, SEQ), lambda b, i: (0, b)),
                  pl.BlockSpec(tab.shape, lambda b, i: (0, 0, 0, 0), pipeline_mode=pl.Buffered(1))],
        out_specs=pl.BlockSpec((None, tq, GROUP_WIDTH), lambda b, i: (b, i, 0)),
        out_shape=jax.ShapeDtypeStruct((nb, SEQ, GROUP_WIDTH), F32),
        compiler_params=_cparams(2),
        name="attn_c",
    )(c_qt, c_k.reshape(nb, SEQ, GROUP_WIDTH), c_vt, tab)
    return o.reshape(nb * SEQ, GROUP_WIDTH)


def _out_ffn_kernel(x_ref, oa1_ref, sa1_ref, oa2_ref, sa2_ref, oa3_ref, sa3_ref, ob_ref, oc_ref, od_ref,
                    expand_ref, og_ref, wout_ref, nf_ref, wg_ref, wu_ref, wd_ref, fin_ref, y_ref,
                    o_scr, s_scr, *, final):
    for j, (dil, o_ref, s_ref) in enumerate(((DIL_MID, oa2_ref, sa2_ref), (DIL_FAR, oa3_ref, sa3_ref))):
        for r in range(dil):
            dst = pl.ds(r, TM_FFN // dil, stride=dil)
            for s in range(GROUP_WIDTH // LANES):
                c0 = r * GROUP_WIDTH + s * LANES
                o_scr[2 * j + s, dst, :] = o_ref[:, c0:c0 + LANES]
            s_scr[j, dst, :] = s_ref[:, r * LANES:(r + 1) * LANES]
    oa = [oa1_ref[...]] + [jnp.concatenate([o_scr[2 * j], o_scr[2 * j + 1]], axis=1) for j in range(2)]

    lses = [sa1_ref[...], s_scr[0], s_scr[1]]
    mx = jnp.maximum(jnp.maximum(lses[0], lses[1]), lses[2])
    es = [jnp.exp2(s - mx) for s in lses]
    den = es[0] + es[1] + es[2]
    expand = expand_ref[...]

    def per_head(w):
        hi = w.astype(BF16)
        lo = (w - hi.astype(F32)).astype(BF16)
        return _dot(hi, expand) + _dot(lo, expand)

    o_a = per_head(es[0] / den) * oa[0] + per_head(es[1] / den) * oa[1] + per_head(es[2] / den) * oa[2]

    x = x_ref[...]
    groups = (o_a, ob_ref[...], oc_ref[...], od_ref[...])
    normed = [_rms(o, og_ref[:, g * GROUP_WIDTH:(g + 1) * GROUP_WIDTH]).astype(BF16) for g, o in enumerate(groups)]
    x1 = x + _dot(jnp.concatenate(normed, axis=1), wout_ref[...])

    h = _rms(x1, nf_ref[...]).astype(BF16)

    def gate_up(c):
        cols = slice(FF_BOUNDS[c], FF_BOUNDS[c + 1])
        gate = _dot(h, wg_ref[:, cols])
        up = _dot(h, wu_ref[:, cols])
        return (gate * (1.0 / (1.0 + jnp.exp(-gate))) * up).astype(BF16)

    n_ff = len(FF_BOUNDS) - 1
    act = gate_up(0)
    ff = None
    for c in range(n_ff):
        act_next = gate_up(c + 1) if c + 1 < n_ff else None
        part = _dot(act, wd_ref[FF_BOUNDS[c]:FF_BOUNDS[c + 1], :])
        ff = part if ff is None else ff + part
        act = act_next
    x2 = x1 + ff
    if final:
        x2 = _rms(x2, fin_ref[...])
    y_ref[...] = x2


def _out_ffn(x2d, attn_outs, lw, expand, final_norm, final):
    t = x2d.shape[0]
    row = lambda i: (i, 0)

    def resident(a):
        return pl.BlockSpec(a.shape, lambda i: (0, 0), pipeline_mode=pl.Buffered(1))

    (oa1, sa1), (oa2, sa2), (oa3, sa3), ob, oc, od = attn_outs
    wide = pl.BlockSpec((TM_FFN, GROUP_WIDTH), row)
    a_specs = [spec for _, d in DILATED_PATTERNS
               for spec in (pl.BlockSpec((TM_FFN // d, d * GROUP_WIDTH), row),
                            pl.BlockSpec((TM_FFN // d, d * LANES), row))]
    weights = (expand, lw["out_gain"], lw["w_out"], lw["norm_ffn"], lw["w_gate"], lw["w_up"], lw["w_down"],
               final_norm)
    return pl.pallas_call(
        functools.partial(_out_ffn_kernel, final=final),
        grid=(t // TM_FFN,),
        in_specs=[pl.BlockSpec((TM_FFN, D_MODEL), row)] + a_specs + [wide, wide, wide]
                 + [resident(w) for w in weights],
        out_specs=pl.BlockSpec((TM_FFN, D_MODEL), row),
        out_shape=jax.ShapeDtypeStruct((t, D_MODEL), F32),
        scratch_shapes=[pltpu.VMEM((2 * GROUP_WIDTH // LANES, TM_FFN, LANES), F32),
                        pltpu.VMEM((2, TM_FFN, LANES), F32)],
        compiler_params=_cparams(1),
        name="out_ffn",
    )(x2d, oa1, sa1, oa2, sa2, oa3, sa3, ob, oc, od, *weights)


def _rope_angles(pos, dim):
    inv = 1.0 / (ROPE_THETA ** (jnp.arange(0, dim, 2, dtype=F32) / dim))
    return pos.astype(F32)[:, None] * inv[None, :]


def _t5_bucket(rel):
    nb = T5_BUCKETS // 2
    max_exact = nb // 2
    n = jnp.abs(rel)
    n_f = jnp.maximum(n, max_exact).astype(F32)
    large = max_exact + (jnp.log(n_f / max_exact) / math.log(T5_MAX_DIST / max_exact)
                         * (nb - max_exact)).astype(jnp.int32)
    large = jnp.minimum(large, nb - 1)
    return jnp.where(rel > 0, nb, 0) + jnp.where(n < max_exact, n, large)


def _tables(t5_bias):
    t = jnp.arange(SEQ, dtype=jnp.int32)
    z16 = jnp.zeros((SEQ, 16), F32)

    def trio(parts):
        return jnp.stack([jnp.concatenate(p, axis=1) for p in parts])

    ang_r = _rope_angles(t // GRID_W, HEAD_DIM // 2)
    ang_c = _rope_angles(t % GRID_W, HEAD_DIM // 2)
    cr, sr, cc, sc = jnp.cos(ang_r), jnp.sin(ang_r), jnp.cos(ang_c), jnp.sin(ang_c)
    rope_b = trio(([cr, cr, cc, cc] * 2, [-sr, z16, -sc, z16] * 2, [z16, sr, z16, sc] * 2))

    ang_t = _rope_angles(t, D_ROPE)
    ct, st = jnp.cos(ang_t), jnp.sin(ang_t)
    ones64, z64, z32 = jnp.ones((SEQ, 64), F32), jnp.zeros((SEQ, 64), F32), jnp.zeros((SEQ, 32), F32)
    rope_d = trio(([ones64, ct, ct, z32], [z64, -st, z16, z32], [z64, z16, st, z32]))

    seg = jnp.kron(jnp.eye(2, dtype=F32), jnp.full((HEAD_DIM, HEAD_DIM), 1.0 / HEAD_DIM, F32)).astype(BF16)
    expand = jnp.zeros((LANES, GROUP_WIDTH), F32).at[:N_HEADS].set(
        jnp.kron(jnp.eye(N_HEADS, dtype=F32), jnp.ones((1, HEAD_DIM), F32))).astype(BF16)

    period = 4 * QBLK_A
    u = np.arange(period)
    rel = np.where(u < period - QBLK_A, u, u - period)
    band = []
    for _, dil in DILATED_PATTERNS:
        off = jnp.arange(-BAND_HALF, BAND_HALF + 1, dtype=jnp.int32) * dil
        bias = t5_bias[_t5_bucket(off)].T.astype(F32)
        variants = []
        for shift in (-BAND_HALF, 0, -2 * BAND_HALF):
            delta = rel + shift
            inside = np.abs(delta) <= BAND_HALF
            diag = jnp.where(inside[None], bias[:, np.clip(delta + BAND_HALF, 0, 2 * BAND_HALF)] * LOG2E,
                             NEG_INF)
            variants.append(_toeplitz(diag, QBLK_A, 2 * QBLK_A))
        band.append(jnp.stack(variants))
    return {"rope_b": rope_b, "rope_d": rope_d, "seg": seg, "expand": expand, "band": band}


def _toeplitz(w, n_rows, n_cols):
    period = w.shape[-1]
    tiled = jnp.tile(w, (1,) * (w.ndim - 1) + (n_rows,))
    return tiled[..., :n_rows * (period - 1)].reshape(w.shape[:-1] + (n_rows, period - 1))[..., :n_cols]


def _na_table(rpb):
    cols = np.arange(GRID_W)
    cs = np.clip(cols - NA_COLS // 2, 0, GRID_W - NA_COLS)
    inside = (cols[None, :] >= cs[:, None]) & (cols[None, :] < cs[:, None] + NA_COLS)
    period = 2 * GRID_W
    u = np.arange(period)
    dc = np.where(u < GRID_W, u, u - period) + NA_COLS - 1
    diag = jnp.where(((dc >= 0) & (dc <= 2 * NA_COLS - 2))[None, None],
                     rpb.astype(F32)[:, :, np.clip(dc, 0, 2 * NA_COLS - 2)], NEG_INF)
    vals = jnp.where(inside[None, None], _toeplitz(diag, GRID_W, GRID_W), NEG_INF)
    vals = jnp.swapaxes(vals, 2, 3) * LOG2E
    masked = jnp.full((N_HEADS, GRID_W, GRID_W), NEG_INF, F32)
    tabs = []
    for lo, step, base in ((0, 1, NA_ROWS // 2 - 1), (0, 0, NA_ROWS - 1), (NA_ROWS // 2, 0, -1)):
        rows = []
        for w in range(WIN_ROWS_C):
            per_a = [vals[:, w - a + base] if lo <= w - a * step < lo + NA_ROWS else masked
                     for a in range(ROWS_C)]
            rows.append(jnp.concatenate(per_a, axis=-1))
        tabs.append(jnp.concatenate(rows, axis=1))
    return jnp.stack(tabs)


def _layer_weights(l, norm_mix, w_in, b_q_gain, b_k_gain, d_q_gain, d_w_uq, d_kv_gain, d_w_ukv,
                   out_gain, w_out, norm_ffn, w_gate, w_up, w_down):
    d_in = w_in.shape[-1]
    kr_lo = d_in - D_ROPE
    w_in_p = jnp.zeros((D_MODEL, D_IN_PAD), F32)
    w_in_p = w_in_p.at[:, :kr_lo].set(w_in[l][:, :kr_lo])
    w_in_p = w_in_p.at[:, kr_lo + D_NOPE:kr_lo + D_NOPE + D_ROPE].set(w_in[l][:, kr_lo:])
    dqk = D_NOPE + D_ROPE
    w_uq = jnp.zeros((D_Q_LORA, N_HEADS, LANES), F32).at[:, :, :dqk].set(
        d_w_uq[l].reshape(D_Q_LORA, N_HEADS, dqk)).reshape(D_Q_LORA, N_HEADS * LANES)
    ukv = d_w_ukv[l].reshape(D_KV_LORA, N_HEADS, D_NOPE + D_V)
    w_uk = jnp.zeros((D_KV_LORA, N_HEADS, LANES), F32).at[:, :, :D_NOPE].set(
        ukv[:, :, :D_NOPE]).reshape(D_KV_LORA, N_HEADS * LANES)
    w_uv = ukv[:, :, D_NOPE:].reshape(D_KV_LORA, N_HEADS * D_V)
    return {
        "norm_mix": norm_mix[l][None, :], "w_in": w_in_p.astype(BF16),
        "b_q_gain": jnp.tile(b_q_gain[l], 2)[None, :], "b_k_gain": jnp.tile(b_k_gain[l], 2)[None, :],
        "d_q_gain": d_q_gain[l][None, :], "w_uq": w_uq.astype(BF16),
        "d_kv_gain": d_kv_gain[l][None, :], "w_uk": w_uk.astype(BF16), "w_uv": w_uv.astype(BF16),
        "out_gain": out_gain[l][None, :], "w_out": w_out[l].astype(BF16), "norm_ffn": norm_ffn[l][None, :],
        "w_gate": w_gate[l].astype(BF16), "w_up": w_up[l].astype(BF16), "w_down": w_down[l].astype(BF16),
    }


def _trunk(x, layers, tabs, na_tabs, final_norm):
    nb = x.shape[0]
    x2d = x.reshape(nb * SEQ, D_MODEL)
    for l, lw in enumerate(layers):
        a1, a4, a16, b_qt, b_k, b_vt, c_qt, c_k, c_vt, d_qt, d_k, d_vt = _proj(x2d, lw, tabs)
        outs_a = tuple(_attn_a(a_dil, tabs["band"][p], nb, dil)
                       for p, (a_dil, (_, dil)) in enumerate(zip((a1, a4, a16), DILATED_PATTERNS)))
        o_b = _attn_b(b_qt, b_k, b_vt, nb)
        o_c = _attn_c(c_qt, c_k, c_vt, na_tabs[l], nb)
        o_d = _attn_d(d_qt, d_k, d_vt, nb)
        x2d = _out_ffn(x2d, outs_a + (o_b, o_c, o_d), lw, tabs["expand"], final_norm, final=(l == DEPTH - 1))
    return x2d.reshape(nb, SEQ, D_MODEL)


def kernel(x_prompt, x_sample, t5_bias, norm_mix, w_in, b_q_gain, b_k_gain, c_rpb, d_q_gain, d_w_uq,
           d_kv_gain, d_w_ukv, out_gain, w_out, norm_ffn, w_gate, w_up, w_down, final_norm):
    tabs = _tables(t5_bias)
    na_tabs = [_na_table(c_rpb[l]) for l in range(DEPTH)]
    layers = [_layer_weights(l, norm_mix, w_in, b_q_gain, b_k_gain, d_q_gain, d_w_uq, d_kv_gain, d_w_ukv,
                             out_gain, w_out, norm_ffn, w_gate, w_up, w_down) for l in range(DEPTH)]
    fin = final_norm[None, :]
    return (_trunk(x_prompt, layers, tabs, na_tabs, fin), _trunk(x_sample, layers, tabs, na_tabs, fin))
```
